```python
import math
import jax, jax.numpy as jnp
from jax import lax
import numpy as np

D_MODEL = 1024
BATCH = 2
SEQ = 8192
DEPTH = 1
DEC_BATCH = 128
DEC_SEQ = 8
PAST_LEN = 8192
PAGE_SIZE = 128

HEAD_DIM = 64
NSA_HEADS = (D_MODEL // 2) // HEAD_DIM
NSA_KV_HEADS = 2
NSA_REP = NSA_HEADS // NSA_KV_HEADS
SB_HEADS = (D_MODEL // 2) // HEAD_DIM
NSA_WIDTH = NSA_HEADS * HEAD_DIM
SB_WIDTH = SB_HEADS * HEAD_DIM
CMP_STRIDE = 16
CMP_LEN = 2 * CMP_STRIDE
CMP_HIDDEN = 64
SEL_BLOCK = 64
N_SEL = 16
WINDOW = 512
REL_BUCKETS = 32
REL_MAX_DIST = 128
Q_BLOCK = 128
N_GROUPS = 4
EXPERTS_PER_GROUP = 8
N_EXPERTS = N_GROUPS * EXPERTS_PER_GROUP
TOP_K = 2
EXPERT_FF = 512
MOE_BLOCK = 128
RMS_EPS = 1e-6
FORCE_SCORE = 1e9

KV_COLS = 2 * NSA_KV_HEADS * HEAD_DIM
OFF_CMP = NSA_WIDTH
OFF_SLC = OFF_CMP + KV_COLS
OFF_WIN = OFF_SLC + KV_COLS
OFF_GATE = OFF_WIN + KV_COLS
OFF_SB = OFF_GATE + 3 * NSA_HEADS
N_IN = OFF_SB + 3 * SB_WIDTH

kernel_name = 'hybrid_nsa_stickbreak_hmoe_step'


def rmsnorm(x, g):
    xf = x.astype(jnp.float32)
    y = xf * lax.rsqrt(jnp.mean(xf * xf, axis=-1, keepdims=True) + RMS_EPS)
    return (y * g.astype(jnp.float32)).astype(x.dtype)


def rel_bucket(dist):
    n = jnp.maximum(dist, 0)
    exact = REL_BUCKETS // 2
    nf = jnp.maximum(n, 1).astype(jnp.float32)
    large = exact + (jnp.log(nf / exact) / math.log(REL_MAX_DIST / exact) * (REL_BUCKETS - exact)).astype(jnp.int32)
    return jnp.where(n < exact, n, jnp.minimum(large, REL_BUCKETS - 1))


def masked_softmax(s, mask):
    s = jnp.where(mask, s.astype(jnp.float32), -jnp.inf)
    m = jnp.max(s, axis=-1, keepdims=True)
    m = jnp.where(jnp.isfinite(m), m, 0.0)
    e = jnp.exp(s - m)
    d = jnp.sum(e, axis=-1, keepdims=True)
    return e / jnp.where(d > 0, d, 1.0)


def split_proj(u):
    lead = u.shape[:-1]
    G = NSA_KV_HEADS
    qa = u[..., :OFF_CMP].reshape(*lead, NSA_HEADS, HEAD_DIM)
    kvc = u[..., OFF_CMP:OFF_SLC].reshape(*lead, 2, G, HEAD_DIM)
    kvs = u[..., OFF_SLC:OFF_WIN].reshape(*lead, 2, G, HEAD_DIM)
    kvw = u[..., OFF_WIN:OFF_GATE].reshape(*lead, 2, G, HEAD_DIM)
    gates = jax.nn.sigmoid(u[..., OFF_GATE:OFF_SB].reshape(*lead, NSA_HEADS, 3))
    sb = u[..., OFF_SB:].reshape(*lead, 3, SB_HEADS, HEAD_DIM)
    return qa, gates, sb[..., 0, :, :], kvc, kvs, kvw, sb[..., 1:, :, :]


def compress_kv(kv, pe, w1, b1, w2):
    n, T = kv.shape[:2]
    G = kv.shape[3]
    n_sub = T // CMP_STRIDE
    sub = kv[:, :n_sub * CMP_STRIDE].reshape(n, n_sub, CMP_STRIDE, 2, G, HEAD_DIM)
    blk = jnp.concatenate([sub[:, :-1], sub[:, 1:]], axis=2)
    blk = blk + jnp.swapaxes(pe, 0, 1)[:, :, None, :]
    nc = blk.shape[1]
    flat = jnp.transpose(blk, (0, 1, 3, 4, 2, 5)).reshape(n, nc, 2, G, CMP_LEN * HEAD_DIM)
    hid = jax.nn.silu(jnp.einsum('ncsgf,sfe->ncsge', flat, w1) + b1[:, None, :])
    out = jnp.einsum('ncsge,sed->ncsgd', hid, w2)
    end = jnp.arange(nc) * CMP_STRIDE + CMP_LEN - 1
    return out, end


def mix_core(qa, gates, qb, qpos, kv_cmp, cmp_end, kv_slc, kv_win, win_pos, kv_sb, rel_bias):
    n, Q = qa.shape[:2]
    G, R = NSA_KV_HEADS, NSA_REP
    dt = qa.dtype
    scale = HEAD_DIM ** -0.5
    qg = qa.reshape(n, Q, G, R, HEAD_DIM)
    bias_grp = rel_bias.reshape(REL_BUCKETS, G, R).astype(jnp.float32)
    ar_g = jnp.arange(G)

    d_c = qpos[:, None] - cmp_end[None, :]
    s_c = jnp.einsum('nqgrd,ncgd->ngrqc', qg, kv_cmp[:, :, 0]).astype(jnp.float32) * scale
    s_c = s_c + jnp.transpose(bias_grp[rel_bucket(d_c)], (2, 3, 0, 1))
    p_c = masked_softmax(s_c, d_c >= 0)
    o_c = jnp.einsum('ngrqc,ncgd->nqgrd', p_c.astype(dt), kv_cmp[:, :, 1])

    Tk = kv_slc.shape[1]
    nsb = -(-Tk // SEL_BLOCK)
    kv_slc = jnp.pad(kv_slc, ((0, 0), (0, nsb * SEL_BLOCK - Tk), (0, 0), (0, 0), (0, 0)))
    c_start = cmp_end - (CMP_LEN - 1)
    b_start = jnp.arange(nsb) * SEL_BLOCK
    overlap = ((c_start[:, None] < b_start[None, :] + SEL_BLOCK) & (cmp_end[:, None] >= b_start[None, :])).astype(jnp.float32)
    imp = jnp.einsum('ngrqc,cj->ngqj', p_c, overlap)
    cur = qpos // SEL_BLOCK
    blk = jnp.arange(nsb)
    forced = (blk[None] == 0) | (blk[None] == cur[:, None]) | (blk[None] == cur[:, None] - 1)
    imp = jnp.where(forced, FORCE_SCORE, jnp.where(blk[None] <= cur[:, None], imp, -jnp.inf))
    k_sel = min(N_SEL, nsb)
    _, idx = lax.top_k(imp, k_sel)
    kv_blocks = jnp.transpose(kv_slc.reshape(n, nsb, SEL_BLOCK, 2, G, HEAD_DIM), (0, 4, 1, 2, 3, 5))
    sel = kv_blocks[jnp.arange(n)[:, None, None, None], ar_g[None, :, None, None], idx]
    X = k_sel * SEL_BLOCK
    sel = sel.reshape(n, G, Q, X, 2, HEAD_DIM)
    kpos = (idx[..., None] * SEL_BLOCK + jnp.arange(SEL_BLOCK)).reshape(n, G, Q, X)
    d_s = qpos[None, None, :, None] - kpos
    b_s = jnp.swapaxes(bias_grp, 0, 1)[ar_g[None, :, None, None], rel_bucket(d_s)]
    s_s = jnp.einsum('nqgrd,ngqxd->ngrqx', qg, sel[..., 0, :]).astype(jnp.float32) * scale
    s_s = s_s + jnp.transpose(b_s, (0, 1, 4, 2, 3))
    p_s = masked_softmax(s_s, (d_s >= 0)[:, :, None])
    o_s = jnp.einsum('ngrqx,ngqxd->nqgrd', p_s.astype(dt), sel[..., 1, :])

    d_w = qpos[:, None] - win_pos[None, :]
    mask_w = (d_w >= 0) & (d_w < WINDOW) & (win_pos[None, :] >= 0)
    s_w = jnp.einsum('nqgrd,nsgd->ngrqs', qg, kv_win[:, :, 0]).astype(jnp.float32) * scale
    s_w = s_w + jnp.transpose(bias_grp[rel_bucket(d_w)], (2, 3, 0, 1))
    p_w = masked_softmax(s_w, mask_w)
    o_w = jnp.einsum('ngrqs,nsgd->nqgrd', p_w.astype(dt), kv_win[:, :, 1])

    gg = gates.reshape(n, Q, G, R, 3, 1)
    o_a = (gg[..., 0, :] * o_c + gg[..., 1, :] * o_s + gg[..., 2, :] * o_w).reshape(n, Q, NSA_WIDTH)

    Tb = kv_sb.shape[1]
    z = jnp.einsum('nqhd,nshd->nhqs', qb, kv_sb[:, :, 0]).astype(jnp.float32) * scale
    mask_b = jnp.arange(Tb)[None, :] < qpos[:, None]
    log_keep = jnp.where(mask_b, jax.nn.log_sigmoid(-z), 0.0)
    log_w = jax.nn.log_sigmoid(z) + lax.cumsum(log_keep, axis=3, reverse=True) - log_keep
    att = jnp.where(mask_b, jnp.exp(log_w), 0.0)
    o_b = jnp.einsum('nhqs,nshd->nqhd', att.astype(dt), kv_sb[:, :, 1]).reshape(n, Q, SB_WIDTH)
    return o_a, o_b


def prompt_mixer(qa, gates, qb, kvc, kvs, kvw, kvsb, pe, w1, b1, w2, rel_bias):
    B, T = qa.shape[:2]
    kv_cmp, cmp_end = compress_kv(kvc, pe, w1, b1, w2)
    nb = T // Q_BLOCK
    kvw_pad = jnp.pad(kvw, ((0, 0), (WINDOW, 0), (0, 0), (0, 0), (0, 0)))

    def blocks(a):
        return jnp.swapaxes(a.reshape(B, nb, Q_BLOCK, *a.shape[2:]), 0, 1)

    def step(args):
        b, qa_b, g_b, qb_b = args
        qpos = b * Q_BLOCK + jnp.arange(Q_BLOCK)
        kv_w = lax.dynamic_slice_in_dim(kvw_pad, b * Q_BLOCK, WINDOW + Q_BLOCK, axis=1)
        win_pos = b * Q_BLOCK - WINDOW + jnp.arange(WINDOW + Q_BLOCK)
        return mix_core(qa_b, g_b, qb_b, qpos, kv_cmp, cmp_end, kvs, kv_w, win_pos, kvsb, rel_bias)

    oa, ob = lax.map(step, (jnp.arange(nb), blocks(qa), blocks(gates), blocks(qb)))
    return jnp.swapaxes(oa, 0, 1).reshape(B, T, NSA_WIDTH), jnp.swapaxes(ob, 0, 1).reshape(B, T, SB_WIDTH)


def sample_mixer(qa, gates, qb, kvc, kvs, kvw, kvsb, cache_sb, cache_cmp, cache_slc, win_buf, page_table, l,
                 pe, w1, b1, w2, rel_bias):
    Bd, S = qa.shape[:2]
    n_pages = page_table.shape[1]
    past = n_pages * PAGE_SIZE
    w_buf = win_buf.shape[1]
    chunk = math.gcd(Bd, max(1, Q_BLOCK // S))
    n_chunks = Bd // chunk
    qpos = past + jnp.arange(S)
    win_pos = past - w_buf + jnp.arange(w_buf + S)

    def chunks(a):
        return a.reshape(n_chunks, chunk, *a.shape[1:])

    def gather(pool, pt):
        rows = pool[l, pt]
        return rows.reshape(chunk, past, *rows.shape[3:])

    def step(args):
        pt, qa_c, g_c, qb_c, kvc_c, kvs_c, kvw_c, kvsb_c, buf_c = args
        kv_cmp, cmp_end = compress_kv(jnp.concatenate([gather(cache_cmp, pt), kvc_c], axis=1), pe, w1, b1, w2)
        return mix_core(qa_c, g_c, qb_c, qpos, kv_cmp, cmp_end,
                        jnp.concatenate([gather(cache_slc, pt), kvs_c], axis=1),
                        jnp.concatenate([buf_c, kvw_c], axis=1), win_pos,
                        jnp.concatenate([gather(cache_sb, pt), kvsb_c], axis=1), rel_bias)

    oa, ob = lax.map(step, (chunks(page_table), chunks(qa), chunks(gates), chunks(qb), chunks(kvc), chunks(kvs),
                            chunks(kvw), chunks(kvsb), chunks(win_buf)))
    return oa.reshape(Bd, S, NSA_WIDTH), ob.reshape(Bd, S, SB_WIDTH)


def moe_ffn(h, rg_w, rg_b, re_w, re_b, w1, w3, w2):
    N, D = h.shape
    dt = h.dtype
    p_grp = jax.nn.softmax((h @ rg_w).astype(jnp.float32) + rg_b, axis=-1)
    grp = jnp.argmax(p_grp, axis=-1)
    grp_w = jnp.take_along_axis(p_grp, grp[:, None], axis=1)
    le = ((h @ re_w).astype(jnp.float32) + re_b).reshape(N, N_GROUPS, EXPERTS_PER_GROUP)
    le = jnp.take_along_axis(le, grp[:, None, None], axis=1)[:, 0]
    top_v, top_i = lax.top_k(le, TOP_K)
    wts = grp_w * jax.nn.softmax(top_v, axis=-1)
    eid = (grp[:, None] * EXPERTS_PER_GROUP + top_i).reshape(-1).astype(jnp.int32)
    A = N * TOP_K
    order = jnp.argsort(eid)
    se = eid[order]
    counts = jnp.bincount(eid, length=N_EXPERTS)
    padded = (counts + MOE_BLOCK - 1) // MOE_BLOCK * MOE_BLOCK
    pad_end = jnp.cumsum(padded)
    pad_start = pad_end - padded
    start = jnp.cumsum(counts) - counts
    dest_sorted = (pad_start[se] + jnp.arange(A) - start[se]).astype(jnp.int32)
    n_blocks = -(-A // MOE_BLOCK) + N_EXPERTS
    row_tok = jnp.full((n_blocks * MOE_BLOCK,), N, jnp.int32).at[dest_sorted].set((order // TOP_K).astype(jnp.int32))
    xb = jnp.concatenate([h, jnp.zeros((1, D), dt)], axis=0)[row_tok].reshape(n_blocks, MOE_BLOCK, D)
    blk_e = jnp.minimum(jnp.searchsorted(pad_end, jnp.arange(n_blocks) * MOE_BLOCK, side='right'), N_EXPERTS - 1)

    def expert_block(args):
        xe, e = args
        return (jax.nn.silu(xe @ w1[e]) * (xe @ w3[e])) @ w2[e]

    yb = lax.map(expert_block, (xb, blk_e)).reshape(n_blocks * MOE_BLOCK, D)
    dest = jnp.zeros((A,), jnp.int32).at[order].set(dest_sorted)
    y = yb[dest].reshape(N, TOP_K, D)
    return jnp.einsum('nk,nkd->nd', wts.astype(dt), y)


def setup_inputs(seed: int = 0) -> dict:
    key = jax.random.key(seed)
    ks = iter(jax.random.split(key, 40))
    f32 = jnp.float32
    D, L = D_MODEL, DEPTH
    nrm = lambda shape, s: jax.random.normal(next(ks), shape, f32) * s
    n_pages = PAST_LEN // PAGE_SIZE
    n_used = DEC_BATCH * n_pages
    n_pool = n_used + (n_used + 3) // 4
    inp = {}
    inp['x_prompt'] = nrm((BATCH, SEQ, D), 1.0)
    inp['x_sample'] = nrm((DEC_BATCH, DEC_SEQ, D), 1.0)
    inp['c_prompt'] = nrm((BATCH, D), 1.0)
    inp['c_sample'] = nrm((DEC_BATCH, D), 1.0)
    inp['cache_sb'] = nrm((L, n_pool, PAGE_SIZE, 2, SB_HEADS, HEAD_DIM), 1.0)
    inp['cache_cmp'] = nrm((L, n_pool, PAGE_SIZE, 2, NSA_KV_HEADS, HEAD_DIM), 1.0)
    inp['cache_slc'] = nrm((L, n_pool, PAGE_SIZE, 2, NSA_KV_HEADS, HEAD_DIM), 1.0)
    inp['cache_win'] = nrm((L, DEC_BATCH, min(WINDOW, PAST_LEN), 2, NSA_KV_HEADS, HEAD_DIM), 1.0)
    inp['page_table'] = jax.random.permutation(next(ks), n_pool)[:n_used].astype(jnp.int32).reshape(DEC_BATCH, n_pages)
    inp['rel_bias'] = nrm((REL_BUCKETS, NSA_HEADS), 0.5)
    inp['final_norm_g'] = 1.0 + nrm((D,), 0.01)
    inp['ada_w'] = nrm((L, D, 6 * D), 0.5 * D ** -0.5)
    inp['ada_b'] = nrm((L, 6 * D), 0.01)
    inp['norm1_g'] = 1.0 + nrm((L, D), 0.01)
    inp['norm2_g'] = 1.0 + nrm((L, D), 0.01)
    inp['w_in'] = nrm((L, D, N_IN), D ** -0.5)
    inp['cmp_pe'] = nrm((L, 2, CMP_LEN, HEAD_DIM), 0.1)
    inp['cmp_w1'] = nrm((L, 2, CMP_LEN * HEAD_DIM, CMP_HIDDEN), (CMP_LEN * HEAD_DIM) ** -0.5)
    inp['cmp_b1'] = nrm((L, 2, CMP_HIDDEN), 0.01)
    inp['cmp_w2'] = nrm((L, 2, CMP_HIDDEN, HEAD_DIM), CMP_HIDDEN ** -0.5)
    inp['out_norm_a'] = 1.0 + nrm((L, NSA_WIDTH), 0.01)
    inp['out_norm_b'] = 1.0 + nrm((L, SB_WIDTH), 0.01)
    inp['w_out'] = nrm((L, D, D), D ** -0.5)
    inp['router_group_w'] = nrm((L, D, N_GROUPS), D ** -0.5)
    inp['router_group_b'] = nrm((L, N_GROUPS), 0.01)
    inp['router_expert_w'] = nrm((L, D, N_EXPERTS), D ** -0.5)
    inp['router_expert_b'] = nrm((L, N_EXPERTS), 0.01)
    inp['expert_w1'] = nrm((L, N_EXPERTS, D, EXPERT_FF), D ** -0.5)
    inp['expert_w3'] = nrm((L, N_EXPERTS, D, EXPERT_FF), D ** -0.5)
    inp['expert_w2'] = nrm((L, N_EXPERTS, EXPERT_FF, D), EXPERT_FF ** -0.5)
    return inp


def reference(x_prompt, x_sample, c_prompt, c_sample, cache_sb, cache_cmp, cache_slc, cache_win, page_table,
              rel_bias, final_norm_g, ada_w, ada_b, norm1_g, norm2_g, w_in, cmp_pe, cmp_w1, cmp_b1, cmp_w2,
              out_norm_a, out_norm_b, w_out, router_group_w, router_group_b, router_expert_w, router_expert_b,
              expert_w1, expert_w3, expert_w2):

    def trunk_layer(x, c, l, mixer):
        sh1, sc1, g1, sh2, sc2, g2 = jnp.split(jax.nn.silu(c) @ ada_w[l] + ada_b[l], 6, axis=-1)
        h = rmsnorm(x, norm1_g[l]) * (1 + sc1[:, None]) + sh1[:, None]
        o_a, o_b, state = mixer(split_proj(h @ w_in[l]), l)
        o = jnp.concatenate([rmsnorm(o_a, out_norm_a[l]), rmsnorm(o_b, out_norm_b[l])], axis=-1) @ w_out[l]
        x = x + g1[:, None] * o
        h2 = rmsnorm(x, norm2_g[l]) * (1 + sc2[:, None]) + sh2[:, None]
        n, t, d = x.shape
        f = moe_ffn(h2.reshape(n * t, d), router_group_w[l], router_group_b[l], router_expert_w[l],
                    router_expert_b[l], expert_w1[l], expert_w3[l], expert_w2[l])
        return x + g2[:, None] * f.reshape(n, t, d), state

    def prompt_mix(parts, l):
        qa, gates, qb, kvc, kvs, kvw, kvsb = parts
        oa, ob = prompt_mixer(qa, gates, qb, kvc, kvs, kvw, kvsb, cmp_pe[l], cmp_w1[l], cmp_b1[l], cmp_w2[l], rel_bias)
        w_keep = min(WINDOW, kvw.shape[1])
        return oa, ob, (kvsb, kvc, kvs, kvw[:, kvw.shape[1] - w_keep:])

    def sample_mix(parts, l):
        qa, gates, qb, kvc, kvs, kvw, kvsb = parts
        buf = cache_win[l]
        oa, ob = sample_mixer(qa, gates, qb, kvc, kvs, kvw, kvsb, cache_sb, cache_cmp, cache_slc, buf, page_table, l,
                              cmp_pe[l], cmp_w1[l], cmp_b1[l], cmp_w2[l], rel_bias)
        new_buf = jnp.concatenate([buf, kvw], axis=1)[:, kvw.shape[1]:]
        return oa, ob, (kvsb, kvc, kvs, new_buf)

    xp, xs = x_prompt, x_sample
    st_p, st_s = [], []
    for l in range(DEPTH):
        xp, sp = trunk_layer(xp, c_prompt, l, prompt_mix)
        xs, ss = trunk_layer(xs, c_sample, l, sample_mix)
        st_p.append(sp)
        st_s.append(ss)
    y_prompt = rmsnorm(xp, final_norm_g)
    y_sample = rmsnorm(xs, final_norm_g)
    sb_p, cmp_p, slc_p, win_p = [jnp.stack(z) for z in zip(*st_p)]
    sb_s, cmp_s, slc_s, win_s = [jnp.stack(z) for z in zip(*st_s)]
    return (y_prompt, y_sample, sb_p, cmp_p, slc_p, win_p, sb_s, cmp_s, slc_s, win_s)
```

```python
import functools
import math

import numpy as np
import jax
import jax.numpy as jnp
from jax import lax
from jax.experimental import pallas as pl
from jax.experimental.pallas import tpu as pltpu

F32 = jnp.float32
BF16 = jnp.bfloat16

HEAD_DIM = 64
NSA_KV_HEADS = 2
NSA_REP = 4
NSA_HEADS = NSA_KV_HEADS * NSA_REP
SB_HEADS = 8
CMP_STRIDE = 16
CMP_LEN = 32
CMP_HIDDEN = 64
SEL_BLOCK = 64
N_SEL = 16
WINDOW = 512
REL_BUCKETS = 32
REL_MAX_DIST = 128
N_GROUPS = 4
EXPERTS_PER_GROUP = 8
N_EXPERTS = N_GROUPS * EXPERTS_PER_GROUP
PAGE_SIZE = 128
RMS_EPS = 1e-6
FORCE_SCORE = 1e9
SCALE = HEAD_DIM ** -0.5

LANES = 128
NEG = -1e30
VMEM_LIMIT = 56 * 1024 * 1024
MOE_ROWS = 256


def _cp(sem, vmem=VMEM_LIMIT):
    return pltpu.CompilerParams(dimension_semantics=sem, vmem_limit_bytes=vmem)


def _nt(a, b):
    return lax.dot_general(a, b, (((1,), (1,)), ((), ())), preferred_element_type=F32)


def _dot(a, b):
    return jnp.dot(a, b, preferred_element_type=F32)


def _split3(x):
    hi = x.astype(BF16)
    r1 = x - hi.astype(F32)
    mid = r1.astype(BF16)
    lo = (r1 - mid.astype(F32)).astype(BF16)
    return hi, mid, lo


def _exact_dot(x, m01):
    hi, mid, lo = _split3(x)
    return _dot(hi, m01) + _dot(mid, m01) + _dot(lo, m01)


def _bucket_np(d):
    n = np.maximum(d, 0)
    exact = REL_BUCKETS // 2
    nf = np.maximum(n, 1).astype(np.float64)
    large = exact + (np.log(nf / exact) / math.log(REL_MAX_DIST / exact) * (REL_BUCKETS - exact)).astype(np.int64)
    return np.where(n < exact, n, np.minimum(large, REL_BUCKETS - 1)).astype(np.int32)


def _bias_by_distance(rel_bias):
    return rel_bias.astype(F32)[_bucket_np(np.arange(REL_MAX_DIST + 1))].T


def _ada_kernel(c_ref, w_ref, b_ref, o_ref):
    c = c_ref[...]
    a = c * (1.0 / (1.0 + jnp.exp(-c)))
    o_ref[...] = _dot(a.astype(BF16), w_ref[...].astype(BF16)) + b_ref[...]


def _ada(c, w, b):
    rows, d = c.shape
    n = w.shape[1]
    tn = 1536
    return pl.pallas_call(
        _ada_kernel,
        out_shape=jax.ShapeDtypeStruct((rows, n), F32),
        grid=(n // tn,),
        in_specs=[pl.BlockSpec((rows, d), lambda j: (0, 0)),
                  pl.BlockSpec((d, tn), lambda j: (0, j)),
                  pl.BlockSpec((1, tn), lambda j: (0, j))],
        out_specs=pl.BlockSpec((rows, tn), lambda j: (0, j)),
        compiler_params=_cp(("arbitrary",)),
        name="ada",
    )(c, w, b.reshape(1, n))


_C_QA, _C_KVC, _C_KVS, _C_KVW, _C_QB, _C_KSB, _C_VSB, _C_GATE, _C_END = 0, 512, 768, 1024, 1280, 1792, 2304, 2816, 2944


def _inproj_kernel(x_ref, sc_ref, sh_ref, g_ref, w_ref,
                   qa_ref, qb_ref, ks_ref, vs_ref, kw_ref, vw_ref, ksb_ref, vsb_ref,
                   kvc_ref, kvs_ref, kvw_ref, kvsb_ref, gate_ref):
    x = x_ref[...]
    nb, r, d = x.shape
    y = x * lax.rsqrt(jnp.mean(x * x, axis=-1, keepdims=True) + RMS_EPS) * g_ref[...]
    h = y * (1.0 + sc_ref[...]) + sh_ref[...]
    u = _dot(h.reshape(nb * r, d).astype(BF16), w_ref[...])
    kvc_ref[...] = u[:, _C_KVC:_C_KVS]
    kvs_ref[...] = u[:, _C_KVS:_C_KVW]
    kvw_ref[...] = u[:, _C_KVW:_C_QB]
    kvsb_ref[...] = u[:, _C_KSB:_C_GATE]
    gate_ref[...] = 1.0 / (1.0 + jnp.exp(-u[:, _C_GATE:_C_END]))
    hd = HEAD_DIM
    for hh in range(NSA_HEADS):
        qa_ref[hh] = (u[:, _C_QA + hh * hd:_C_QA + (hh + 1) * hd] * SCALE).astype(BF16)
    for hh in range(SB_HEADS):
        qb_ref[hh] = (u[:, _C_QB + hh * hd:_C_QB + (hh + 1) * hd] * SCALE).astype(BF16)
        ksb_ref[hh] = u[:, _C_KSB + hh * hd:_C_KSB + (hh + 1) * hd].astype(BF16)
        vsb_ref[hh] = u[:, _C_VSB + hh * hd:_C_VSB + (hh + 1) * hd].astype(BF16)
    g2 = NSA_KV_HEADS * hd
    for gg in range(NSA_KV_HEADS):
        ks_ref[gg] = u[:, _C_KVS + gg * hd:_C_KVS + (gg + 1) * hd].astype(BF16)
        vs_ref[gg] = u[:, _C_KVS + g2 + gg * hd:_C_KVS + g2 + (gg + 1) * hd].astype(BF16)
        kw_ref[gg] = u[:, _C_KVW + gg * hd:_C_KVW + (gg + 1) * hd].astype(BF16)
        vw_ref[gg] = u[:, _C_KVW + g2 + gg * hd:_C_KVW + g2 + (gg + 1) * hd].astype(BF16)


def _inproj(x3, sc, sh, g, w_perm, nb, r):
    NB, R, D = x3.shape
    n_tok = NB * R
    rows = nb * r
    grid = (NB // nb, R // r)
    hd = HEAD_DIM

    def row_map(i, j):
        return (i * (R // r) + j, 0)

    def hm_map(i, j):
        return (0, i * (R // r) + j, 0)

    def hm(nh):
        return jax.ShapeDtypeStruct((nh, n_tok, hd), BF16), pl.BlockSpec((nh, rows, hd), hm_map)

    def flat(nc):
        return jax.ShapeDtypeStruct((n_tok, nc), F32), pl.BlockSpec((rows, nc), row_map)

    outs = [hm(NSA_HEADS), hm(SB_HEADS), hm(NSA_KV_HEADS), hm(NSA_KV_HEADS), hm(NSA_KV_HEADS), hm(NSA_KV_HEADS),
            hm(SB_HEADS), hm(SB_HEADS), flat(256), flat(256), flat(256), flat(1024), flat(LANES)]
    return pl.pallas_call(
        _inproj_kernel,
        out_shape=[o[0] for o in outs],
        grid=grid,
        in_specs=[pl.BlockSpec((nb, r, D), lambda i, j: (i, j, 0)),
                  pl.BlockSpec((nb, 1, D), lambda i, j: (i, 0, 0)),
                  pl.BlockSpec((nb, 1, D), lambda i, j: (i, 0, 0)),
                  pl.BlockSpec((1, 1, D), lambda i, j: (0, 0, 0)),
                  pl.BlockSpec(w_perm.shape, lambda i, j: (0, 0))],
        out_specs=[o[1] for o in outs],
        compiler_params=_cp(("arbitrary", "arbitrary")),
        name="inproj",
    )(x3, sc, sh, g.reshape(1, 1, D), w_perm)


def _compress_weights(pe, w1, b1, w2):
    G = NSA_KV_HEADS
    half = CMP_STRIDE
    w1r = w1.reshape(2, CMP_LEN, HEAD_DIM, CMP_HIDDEN)
    eye_s = jnp.eye(2, dtype=F32)
    eye_g = jnp.eye(G, dtype=F32)

    def expand(wpart):
        m = jnp.einsum('spde,ts,hg->pthdsge', wpart, eye_s, eye_g)
        return m.reshape(half * 2 * G * HEAD_DIM, 2 * G * CMP_HIDDEN).astype(BF16)

    wa = expand(w1r[:, :half])
    wb = expand(w1r[:, half:])
    pe_t = jnp.transpose(pe, (1, 0, 2))
    pe_e = jnp.broadcast_to(pe_t[:, :, None, :], (CMP_LEN, 2, G, HEAD_DIM))
    pea = pe_e[:half].reshape(1, -1).astype(F32)
    peb = pe_e[half:].reshape(1, -1).astype(F32)
    b1e = jnp.broadcast_to(b1[:, None, :], (2, G, CMP_HIDDEN)).reshape(1, -1).astype(F32)
    w2bd = jnp.einsum('sed,ts,hg->thesgd', w2, eye_s, eye_g).reshape(2 * G * CMP_HIDDEN, 2 * G * HEAD_DIM).astype(BF16)
    return wa, wb, pea, peb, b1e, w2bd


def _compress_tail(hid_a, hid_b, b1e, w2bd, kc_ref, vc_ref):
    n_sub = hid_a.shape[0]
    hid = hid_a + pltpu.roll(hid_b, n_sub - 1, 0) + b1e
    act = hid * (1.0 / (1.0 + jnp.exp(-hid)))
    out = _dot(act.astype(BF16), w2bd)
    hd = HEAD_DIM
    for gg in range(NSA_KV_HEADS):
        kc_ref[gg] = out[:, gg * hd:(gg + 1) * hd].astype(BF16)
        vc_ref[gg] = out[:, (NSA_KV_HEADS + gg) * hd:(NSA_KV_HEADS + gg + 1) * hd].astype(BF16)


def _compress_prompt_kernel(x_ref, wa_ref, wb_ref, pea_ref, peb_ref, b1_ref, w2_ref, kc_ref, vc_ref):
    x = x_ref[0]
    hid_a = _dot((x + pea_ref[...]).astype(BF16), wa_ref[...])
    hid_b = _dot((x + peb_ref[...]).astype(BF16), wb_ref[...])
    _compress_tail(hid_a, hid_b, b1_ref[...], w2_ref[...], kc_ref, vc_ref)


def _compress_prompt(kvc, B, T, cw):
    wa, wb, pea, peb, b1e, w2bd = cw
    n_sub = T // CMP_STRIDE
    kdim = CMP_STRIDE * 256
    x = kvc.reshape(B, n_sub, kdim)
    G = NSA_KV_HEADS
    full = lambda a: pl.BlockSpec(a.shape, lambda b: (0,) * a.ndim)
    out_sd = jax.ShapeDtypeStruct((G, B * n_sub, HEAD_DIM), BF16)
    out_spec = pl.BlockSpec((G, n_sub, HEAD_DIM), lambda b: (0, b, 0))
    return pl.pallas_call(
        _compress_prompt_kernel,
        out_shape=[out_sd, out_sd],
        grid=(B,),
        in_specs=[pl.BlockSpec((1, n_sub, kdim), lambda b: (b, 0, 0)),
                  full(wa), full(wb), full(pea), full(peb), full(b1e), full(w2bd)],
        out_specs=[out_spec, out_spec],
        compiler_params=_cp(("arbitrary",)),
        name="compress_prompt",
    )(x, wa, wb, pea, peb, b1e, w2bd)


def _topk_mask(imp, k):
    lane = lax.broadcasted_iota(jnp.int32, imp.shape, 1)

    def body(_, carry):
        avail, sel = carry
        cur = jnp.where(avail > 0, imp, -jnp.inf)
        m = jnp.max(cur, axis=-1, keepdims=True)
        cand = jnp.where((avail > 0) & (cur == m), lane, LANES)
        idx = jnp.min(cand, axis=-1, keepdims=True)
        hit = lane == idx
        return jnp.where(hit, 0.0, avail), jnp.where(hit, 1.0, sel)

    _, sel = lax.fori_loop(0, k, body, (jnp.ones(imp.shape, F32), jnp.zeros(imp.shape, F32)))
    return sel


def _flash_step(carry, s, valid, v):
    m, l, acc = carry
    s = jnp.where(valid, s, NEG)
    m_new = jnp.maximum(m, jnp.max(s, axis=-1, keepdims=True))
    alpha = jnp.exp(m - m_new)
    p = jnp.where(valid, jnp.exp(s - m_new), 0.0)
    l = l * alpha + jnp.sum(p, axis=-1, keepdims=True)
    acc = acc * alpha + _dot(p.astype(BF16), v)
    return m_new, l, acc


def _flash_init(rows):
    return (jnp.full((rows, 1), NEG, F32), jnp.zeros((rows, 1), F32), jnp.zeros((rows, HEAD_DIM), F32))


def _flash_out(carry):
    _, l, acc = carry
    return acc / jnp.where(l > 0, l, 1.0)


_BAND = 16
_BAND_LO = 9
_TQ = 128


def _nsa_prompt_tables(rel_bias):
    tb = _bias_by_distance(rel_bias)
    qi = np.arange(_TQ)[:, None]
    idx_band = np.clip(qi - CMP_STRIDE * (np.arange(_BAND)[None, :] - _BAND_LO) - (CMP_LEN - 1), 0, REL_MAX_DIST)
    band = tb[:, idx_band]
    hi, mid, lo = _split3(band)
    band3 = jnp.concatenate([hi, mid, lo], axis=-1).reshape(NSA_HEADS * _TQ, 3 * _BAND)
    ki = np.arange(_TQ)[None, :]
    toep = jnp.stack([tb[:, np.clip(qi - ki, 0, REL_MAX_DIST)],
                      tb[:, np.clip(_TQ + qi - ki, 0, REL_MAX_DIST)]], axis=1)
    cb = tb[:, REL_MAX_DIST].reshape(NSA_HEADS, 1, 1)
    return band3, toep, cb


def _overlap_np(n_sub):
    c = np.arange(n_sub)[:, None]
    j = np.arange(LANES)[None, :]
    c_start = c * CMP_STRIDE
    c_end = c_start + CMP_LEN - 1
    ov = (c_start < j * SEL_BLOCK + SEL_BLOCK) & (c_end >= j * SEL_BLOCK) & (c < n_sub - 1)
    return ov.astype(np.float32)


def _nsa_prompt_kernel(q_ref, kc_ref, vc_ref, ks_ref, vs_ref, kw_ref, vw_ref, gate_ref,
                       band_ref, toep_ref, cb_ref, ov_ref, o_ref, *, n_sub, k_sel):
    R = NSA_REP
    tq = _TQ
    rows = R * tq
    qb = pl.program_id(2)
    g = pl.program_id(1)
    q0 = qb * tq
    q = q_ref[...].reshape(rows, HEAD_DIM)
    cb = cb_ref[...]
    qi3 = lax.broadcasted_iota(jnp.int32, (R, tq, 1), 1)

    s = _nt(q, kc_ref[0]).reshape(R, tq, n_sub)
    lo = qb * (tq // CMP_STRIDE) - _BAND_LO
    r_io = lax.broadcasted_iota(jnp.int32, (3 * _BAND, n_sub), 0)
    c_io = lax.broadcasted_iota(jnp.int32, (3 * _BAND, n_sub), 1)
    shift = jnp.where(c_io == lo + r_io % _BAND, 1.0, 0.0).astype(BF16)
    bias_band = _dot(band_ref[...], shift).reshape(R, tq, n_sub)
    c3 = lax.broadcasted_iota(jnp.int32, (R, tq, n_sub), 2)
    valid = (c3 * CMP_STRIDE + (CMP_LEN - 1)) <= (q0 + qi3)
    valid = valid & (c3 < n_sub - 1)
    s = jnp.where(valid, s + jnp.where(c3 < lo, cb, bias_band), NEG)
    m = jnp.max(s, axis=-1, keepdims=True)
    e = jnp.where(valid, jnp.exp(s - m), 0.0)
    den = jnp.sum(e, axis=-1, keepdims=True)
    p_c = e / jnp.where(den > 0, den, 1.0)
    o_c = _dot(p_c.reshape(rows, n_sub).astype(BF16), vc_ref[0])

    imp = _exact_dot(jnp.sum(p_c, axis=0), ov_ref[...])
    blk = lax.broadcasted_iota(jnp.int32, (tq, LANES), 1)
    cur = (q0 + lax.broadcasted_iota(jnp.int32, (tq, LANES), 0)) // SEL_BLOCK
    forced = (blk == 0) | (blk == cur) | (blk == cur - 1)
    imp = jnp.where(forced, FORCE_SCORE, jnp.where(blk <= cur, imp, -jnp.inf))
    sel = _topk_mask(imp, k_sel).astype(BF16)

    qi = lax.broadcasted_iota(jnp.int32, (R, tq, tq), 1)
    ki = lax.broadcasted_iota(jnp.int32, (R, tq, tq), 2)
    causal = ki <= qi
    e_r = lax.broadcasted_iota(jnp.int32, (LANES, tq), 0)
    e_c = lax.broadcasted_iota(jnp.int32, (LANES, tq), 1) // SEL_BLOCK
    per_tile = tq // SEL_BLOCK

    def sel_keys(kt):
        expand = jnp.where(e_r == kt * per_tile + e_c, 1.0, 0.0).astype(BF16)
        return (_dot(sel, expand) > 0.5)[None]

    def tile_logits(k_ref, kt):
        k = k_ref[0, pl.ds(pl.multiple_of(kt * tq, tq), tq), :]
        return _nt(q, k).reshape(R, tq, tq)

    def tile_v(v_ref, kt):
        return v_ref[0, pl.ds(pl.multiple_of(kt * tq, tq), tq), :]

    def fstep(carry, s3, valid3, v):
        return _flash_step(carry, s3.reshape(rows, tq), jnp.broadcast_to(valid3, (R, tq, tq)).reshape(rows, tq), v)

    def slc_body(kt, carry):
        bias = jnp.where(kt == qb - 1, toep_ref[:, 1], cb)
        return fstep(carry, tile_logits(ks_ref, kt) + bias, sel_keys(kt), tile_v(vs_ref, kt))

    carry = lax.fori_loop(0, qb, slc_body, _flash_init(rows))
    carry = fstep(carry, tile_logits(ks_ref, qb) + toep_ref[:, 0], sel_keys(qb) & causal, tile_v(vs_ref, qb))
    o_s = _flash_out(carry)

    n_back = WINDOW // tq
    carry = _flash_init(rows)
    for j in range(n_back, -1, -1):
        kt = jnp.maximum(qb - j, 0)
        live = ki >= jnp.where(qb - j >= 0, 0, tq)
        if j == 0:
            bias, valid3 = toep_ref[:, 0], causal
        elif j == 1:
            bias, valid3 = toep_ref[:, 1], live
        elif j == n_back:
            bias, valid3 = cb, (ki > qi) & live
        else:
            bias, valid3 = cb, live
        carry = fstep(carry, tile_logits(kw_ref, kt) + bias, valid3, tile_v(vw_ref, kt))
    o_w = _flash_out(carry)

    gts = gate_ref[...]
    for r in range(R):
        hh = g * R + r
        lane = lax.broadcasted_iota(jnp.int32, (tq, LANES), 1)

        def gate(b):
            return jnp.sum(jnp.where(lane == hh * 3 + b, gts, 0.0), axis=-1, keepdims=True)

        sl = slice(r * tq, (r + 1) * tq)
        o_ref[r] = gate(0) * o_c[sl] + gate(1) * o_s[sl] + gate(2) * o_w[sl]


def _nsa_prompt(qa, kc, vc, ks, vs, kw, vw, gates, tables, B, T):
    band3, toep, cb = tables
    G, R = NSA_KV_HEADS, NSA_REP
    n_sub = T // CMP_STRIDE
    nq = T // _TQ
    ov = jnp.asarray(_overlap_np(n_sub), BF16)
    k_sel = min(N_SEL, -(-T // SEL_BLOCK))
    kern = functools.partial(_nsa_prompt_kernel, n_sub=n_sub, k_sel=k_sel)
    kv_spec = pl.BlockSpec((1, T, HEAD_DIM), lambda b, g, i: (g, b, 0))
    cmp_spec = pl.BlockSpec((1, n_sub, HEAD_DIM), lambda b, g, i: (g, b, 0))
    return pl.pallas_call(
        kern,
        out_shape=jax.ShapeDtypeStruct((NSA_HEADS, B * T, HEAD_DIM), F32),
        grid=(B, G, nq),
        in_specs=[pl.BlockSpec((R, _TQ, HEAD_DIM), lambda b, g, i: (g, b * nq + i, 0)),
                  cmp_spec, cmp_spec, kv_spec, kv_spec, kv_spec, kv_spec,
                  pl.BlockSpec((_TQ, LANES), lambda b, g, i: (b * nq + i, 0)),
                  pl.BlockSpec((R * _TQ, 3 * _BAND), lambda b, g, i: (g, 0)),
                  pl.BlockSpec((R, 2, _TQ, _TQ), lambda b, g, i: (g, 0, 0, 0)),
                  pl.BlockSpec((R, 1, 1), lambda b, g, i: (g, 0, 0)),
                  pl.BlockSpec(ov.shape, lambda b, g, i: (0, 0))],
        out_specs=pl.BlockSpec((R, _TQ, HEAD_DIM), lambda b, g, i: (g, b * nq + i, 0)),
        compiler_params=_cp(("arbitrary", "arbitrary", "arbitrary")),
        name="nsa_prompt",
    )(qa, kc, vc, ks, vs, kw, vw, gates, band3, toep, cb, ov)


_SB_T = 256


def _sb_tile(q, k, v, tri, suf, acc, mask):
    z = _nt(q, k)
    lsz = jnp.minimum(z, 0.0) - jnp.log(1.0 + jnp.exp(-jnp.abs(z)))
    lk = lsz - z
    if mask is not None:
        lk = jnp.where(mask, lk, 0.0)
    att = jnp.exp(lsz + _exact_dot(lk, tri) + suf)
    if mask is not None:
        att = jnp.where(mask, att, 0.0)
    acc = acc + _dot(att.astype(BF16), v)
    suf = suf + jnp.sum(lk, axis=-1, keepdims=True)
    return suf, acc


def _sb_prompt_kernel(q_ref, k_ref, v_ref, tri_ref, o_ref):
    t = _SB_T
    qb = pl.program_id(2)
    q = q_ref[0]
    tri = tri_ref[...]
    qi = lax.broadcasted_iota(jnp.int32, (t, t), 0)
    ki = lax.broadcasted_iota(jnp.int32, (t, t), 1)

    def kv(kt):
        sl = pl.ds(pl.multiple_of(kt * t, t), t)
        return k_ref[0, sl, :], v_ref[0, sl, :]

    k, v = kv(qb)
    suf, acc = _sb_tile(q, k, v, tri, jnp.zeros((t, 1), F32), jnp.zeros((t, HEAD_DIM), F32), ki < qi)

    def body(i, carry):
        k, v = kv(qb - 1 - i)
        return _sb_tile(q, k, v, tri, carry[0], carry[1], None)

    _, acc = lax.fori_loop(0, qb, body, (suf, acc))
    o_ref[0] = acc


def _suffix_matrix(t):
    i = np.arange(t)
    return jnp.asarray((i[:, None] > i[None, :]).astype(np.float32), BF16)


def _sb_prompt(qb, ksb, vsb, B, T):
    t = min(_SB_T, T)
    nq = T // t
    kv_spec = pl.BlockSpec((1, T, HEAD_DIM), lambda b, h, i: (h, b, 0))
    q_spec = pl.BlockSpec((1, t, HEAD_DIM), lambda b, h, i: (h, b * nq + i, 0))
    tri = _suffix_matrix(t)
    return pl.pallas_call(
        _sb_prompt_kernel,
        out_shape=jax.ShapeDtypeStruct((SB_HEADS, B * T, HEAD_DIM), F32),
        grid=(B, SB_HEADS, nq),
        in_specs=[q_spec, kv_spec, kv_spec, pl.BlockSpec(tri.shape, lambda b, h, i: (0, 0))],
        out_specs=q_spec,
        compiler_params=_cp(("arbitrary", "arbitrary", "arbitrary")),
        name="sb_prompt",
    )(qb, ksb, vsb, tri)


def _outproj_kernel(oa_ref, ob_ref, x_ref, g1_ref, sc_ref, sh_ref, ga_ref, gb_ref, n2_ref, w_ref, wr_ref, br_ref,
                    x1_ref, h2_ref, route_ref, oh_ref, cnt_ref):
    nb, r, d = x_ref.shape
    rows = nb * r

    def normed(o_ref, g_ref):
        o = o_ref[...]
        ssq = jnp.sum(jnp.sum(o * o, axis=-1, keepdims=True), axis=0)
        rs = lax.rsqrt(ssq / (o.shape[0] * HEAD_DIM) + RMS_EPS)
        return [(o[hh] * rs * g_ref[hh]).astype(BF16) for hh in range(o.shape[0])]

    parts = normed(oa_ref, ga_ref) + normed(ob_ref, gb_ref)
    o = _dot(parts[0], w_ref[0])
    for hh in range(1, len(parts)):
        o = o + _dot(parts[hh], w_ref[hh])
    x1 = x_ref[...] + g1_ref[...] * o.reshape(nb, r, d)
    y = x1 * lax.rsqrt(jnp.mean(x1 * x1, axis=-1, keepdims=True) + RMS_EPS) * n2_ref[...]
    h2 = (y * (1.0 + sc_ref[...]) + sh_ref[...]).reshape(rows, d)
    x1_ref[...] = x1.reshape(rows, d)
    h2_ref[...] = h2

    logits = _dot(h2.astype(BF16), wr_ref[...]) + br_ref[...]
    lane = lax.broadcasted_iota(jnp.int32, (rows, LANES), 1)
    is_g = lane < N_GROUPS
    lg = jnp.where(is_g, logits, -jnp.inf)
    eg = jnp.where(is_g, jnp.exp(lg - jnp.max(lg, axis=-1, keepdims=True)), 0.0)
    pg = eg / jnp.sum(eg, axis=-1, keepdims=True)
    grp_w = jnp.max(pg, axis=-1, keepdims=True)
    grp = jnp.min(jnp.where(is_g & (pg == grp_w), lane, LANES), axis=-1, keepdims=True)
    base = N_GROUPS + grp * EXPERTS_PER_GROUP
    in_grp = (lane >= base) & (lane < base + EXPERTS_PER_GROUP)
    l1 = jnp.where(in_grp, logits, -jnp.inf)
    v1 = jnp.max(l1, axis=-1, keepdims=True)
    i1 = jnp.min(jnp.where(in_grp & (l1 == v1), lane, LANES), axis=-1, keepdims=True)
    l2 = jnp.where(in_grp & (lane != i1), logits, -jnp.inf)
    v2 = jnp.max(l2, axis=-1, keepdims=True)
    i2 = jnp.min(jnp.where(in_grp & (lane != i1) & (l2 == v2), lane, LANES), axis=-1, keepdims=True)
    e2 = jnp.exp(v2 - v1)
    w1 = grp_w * (1.0 / (1.0 + e2))
    w2 = grp_w * (e2 / (1.0 + e2))
    e1 = (i1 - N_GROUPS).astype(F32)
    e2i = (i2 - N_GROUPS).astype(F32)
    route_ref[...] = jnp.where(lane == 0, e1, jnp.where(lane == 1, e2i, jnp.where(lane == 2, w1, jnp.where(lane == 3, w2, 0.0))))
    oh = jnp.where((lane == i1 - N_GROUPS) | (lane == i2 - N_GROUPS), 1.0, 0.0)
    oh_ref[...] = oh.astype(BF16)

    @pl.when((pl.program_id(0) == 0) & (pl.program_id(1) == 0))
    def _():
        cnt_ref[...] = jnp.zeros_like(cnt_ref)

    cnt_ref[...] += jnp.sum(oh, axis=0, keepdims=True)


def _outproj(oa, ob, x3, g1, sc2, sh2, ga, gb, n2g, w_out_h, wr, br, nb, r):
    NB, R, D = x3.shape
    n_tok = NB * R
    rows = nb * r
    grid = (NB // nb, R // r)
    row_map = lambda i, j: (i * (R // r) + j, 0)
    hm_map = lambda i, j: (0, i * (R // r) + j, 0)
    mod_spec = pl.BlockSpec((nb, 1, D), lambda i, j: (i, 0, 0))
    full = lambda a: pl.BlockSpec(a.shape, lambda i, j: (0,) * a.ndim)
    return pl.pallas_call(
        _outproj_kernel,
        out_shape=[jax.ShapeDtypeStruct((n_tok, D), F32), jax.ShapeDtypeStruct((n_tok, D), F32),
                   jax.ShapeDtypeStruct((n_tok, LANES), F32), jax.ShapeDtypeStruct((n_tok, LANES), BF16),
                   jax.ShapeDtypeStruct((1, LANES), F32)],
        grid=grid,
        in_specs=[pl.BlockSpec((NSA_HEADS, rows, HEAD_DIM), hm_map), pl.BlockSpec((SB_HEADS, rows, HEAD_DIM), hm_map),
                  pl.BlockSpec((nb, r, D), lambda i, j: (i, j, 0)), mod_spec, mod_spec, mod_spec,
                  full(ga), full(gb), full(n2g), full(w_out_h), full(wr), full(br)],
        out_specs=[pl.BlockSpec((rows, D), row_map), pl.BlockSpec((rows, D), row_map),
                   pl.BlockSpec((rows, LANES), row_map), pl.BlockSpec((rows, LANES), row_map),
                   pl.BlockSpec((1, LANES), lambda i, j: (0, 0))],
        compiler_params=_cp(("arbitrary", "arbitrary")),
        name="outproj_router",
    )(oa, ob, x3, g1, sc2, sh2, ga, gb, n2g, w_out_h, wr, br)


_RANK_T = 512


def _rank_kernel(oh_ref, route_ref, start_ref, lower_ref, dest_ref, run_ref):
    @pl.when(pl.program_id(0) == 0)
    def _():
        run_ref[...] = jnp.zeros_like(run_ref)

    oh = oh_ref[...]
    pos = _dot(lower_ref[...], oh) + run_ref[...] + start_ref[...]
    route = route_ref[...]
    lane = lax.broadcasted_iota(jnp.int32, oh.shape, 1)

    def pick(col):
        e = jnp.sum(jnp.where(lane == col, route, 0.0), axis=-1, keepdims=True).astype(jnp.int32)
        return jnp.sum(jnp.where(lane == e, pos, 0.0), axis=-1, keepdims=True)

    dest_ref[...] = jnp.where(lane == 0, pick(0), jnp.where(lane == 1, pick(1), 0.0))
    run_ref[...] += jnp.sum(oh.astype(F32), axis=0, keepdims=True)


def _rank(oh, route, pad_start):
    n = oh.shape[0]
    t = _RANK_T
    i = np.arange(t)
    lower = jnp.asarray((i[None, :] < i[:, None]).astype(np.float32), BF16)
    return pl.pallas_call(
        _rank_kernel,
        out_shape=jax.ShapeDtypeStruct((n, LANES), F32),
        grid=(n // t,),
        in_specs=[pl.BlockSpec((t, LANES), lambda i: (i, 0)), pl.BlockSpec((t, LANES), lambda i: (i, 0)),
                  pl.BlockSpec((1, LANES), lambda i: (0, 0)), pl.BlockSpec((t, t), lambda i: (0, 0))],
        out_specs=pl.BlockSpec((t, LANES), lambda i: (i, 0)),
        scratch_shapes=[pltpu.VMEM((1, LANES), F32)],
        compiler_params=_cp(("arbitrary",)),
        name="moe_rank",
    )(oh, route, pad_start, lower)


_SCAT_T = 512


def _scatter_kernel(dest_ref, h_ref, xb_in_ref, xb_ref, sem):
    del xb_in_ref
    base = pl.program_id(0) * _SCAT_T

    def row_copy(tok, d):
        return pltpu.make_async_copy(h_ref.at[pl.ds(tok, 1)], xb_ref.at[pl.ds(d, 1)], sem)

    def issue(t, c):
        row_copy(base + t, dest_ref[2 * t]).start()
        row_copy(base + t, dest_ref[2 * t + 1]).start()
        return c

    lax.fori_loop(0, _SCAT_T, issue, 0)

    def drain(t, c):
        row_copy(0, 0).wait()
        return c

    lax.fori_loop(0, 2 * _SCAT_T, drain, 0)


def _scatter(dest_flat, h2, n_rows):
    n, d = h2.shape
    xb0 = jnp.zeros((n_rows, d), h2.dtype)
    return pl.pallas_call(
        _scatter_kernel,
        out_shape=jax.ShapeDtypeStruct((n_rows, d), h2.dtype),
        grid=(n // _SCAT_T,),
        in_specs=[pl.BlockSpec((2 * _SCAT_T,), lambda i: (i,), memory_space=pltpu.SMEM),
                  pl.BlockSpec(memory_space=pl.ANY), pl.BlockSpec(memory_space=pl.ANY)],
        out_specs=pl.BlockSpec(memory_space=pl.ANY),
        scratch_shapes=[pltpu.SemaphoreType.DMA(())],
        input_output_aliases={2: 0},
        compiler_params=pltpu.CompilerParams(dimension_semantics=("arbitrary",), has_side_effects=True),
        name="moe_scatter",
    )(dest_flat, h2, xb0)


def _expert_kernel(be_ref, bv_ref, x_ref, w1_ref, w3_ref, w2_ref, y_ref):
    i = pl.program_id(0)

    @pl.when(bv_ref[i] > 0)
    def _():
        x = x_ref[...].astype(BF16)
        a = _dot(x, w1_ref[0].astype(BF16))
        b = _dot(x, w3_ref[0].astype(BF16))
        hmid = a * (1.0 / (1.0 + jnp.exp(-a))) * b
        y_ref[...] = _dot(hmid.astype(BF16), w2_ref[0].astype(BF16))

    @pl.when(bv_ref[i] == 0)
    def _():
        y_ref[...] = jnp.zeros_like(y_ref)


def _experts(blk_e, blk_valid, xb, w1, w3, w2):
    n_rows, d = xb.shape
    ff = w1.shape[2]
    n_blocks = n_rows // MOE_ROWS
    gs = pltpu.PrefetchScalarGridSpec(
        num_scalar_prefetch=2,
        grid=(n_blocks,),
        in_specs=[pl.BlockSpec((MOE_ROWS, d), lambda i, be, bv: (i, 0)),
                  pl.BlockSpec((1, d, ff), lambda i, be, bv: (be[i], 0, 0)),
                  pl.BlockSpec((1, d, ff), lambda i, be, bv: (be[i], 0, 0)),
                  pl.BlockSpec((1, ff, d), lambda i, be, bv: (be[i], 0, 0))],
        out_specs=pl.BlockSpec((MOE_ROWS, d), lambda i, be, bv: (i, 0)),
    )
    return pl.pallas_call(
        _expert_kernel,
        out_shape=jax.ShapeDtypeStruct((n_rows, d), F32),
        grid_spec=gs,
        compiler_params=_cp(("arbitrary",)),
        name="moe_experts",
    )(blk_e, blk_valid, xb, w1, w3, w2)


def _combine_kernel(dest_ref, route_ref, yb_ref, f_ref, buf0, buf1, sem):
    def row_copy(d, buf, t):
        return pltpu.make_async_copy(yb_ref.at[pl.ds(d, 1)], buf.at[pl.ds(t, 1)], sem)

    def issue(t, c):
        row_copy(dest_ref[2 * t], buf0, t).start()
        row_copy(dest_ref[2 * t + 1], buf1, t).start()
        return c

    lax.fori_loop(0, _SCAT_T, issue, 0)

    def drain(t, c):
        row_copy(0, buf0, 0).wait()
        return c

    lax.fori_loop(0, 2 * _SCAT_T, drain, 0)
    route = route_ref[...]
    lane = lax.broadcasted_iota(jnp.int32, route.shape, 1)
    w1 = jnp.sum(jnp.where(lane == 2, route, 0.0), axis=-1, keepdims=True)
    w2 = jnp.sum(jnp.where(lane == 3, route, 0.0), axis=-1, keepdims=True)
    f_ref[...] = w1 * buf0[...] + w2 * buf1[...]


def _combine(dest_flat, route, yb):
    n = route.shape[0]
    d = yb.shape[1]
    return pl.pallas_call(
        _combine_kernel,
        out_shape=jax.ShapeDtypeStruct((n, d), F32),
        grid=(n // _SCAT_T,),
        in_specs=[pl.BlockSpec((2 * _SCAT_T,), lambda i: (i,), memory_space=pltpu.SMEM),
                  pl.BlockSpec((_SCAT_T, LANES), lambda i: (i, 0)),
                  pl.BlockSpec(memory_space=pl.ANY)],
        out_specs=pl.BlockSpec((_SCAT_T, d), lambda i: (i, 0)),
        scratch_shapes=[pltpu.VMEM((_SCAT_T, d), F32), pltpu.VMEM((_SCAT_T, d), F32), pltpu.SemaphoreType.DMA(())],
        compiler_params=_cp(("arbitrary",)),
        name="moe_combine",
    )(dest_flat, route, yb)


def _moe(h2, route, oh, counts, w1, w3, w2):
    n = h2.shape[0]
    cnt = counts[0, :N_EXPERTS].astype(jnp.int32)
    nblk = (cnt + MOE_ROWS - 1) // MOE_ROWS
    blk_end = jnp.cumsum(nblk)
    pad_start = ((blk_end - nblk) * MOE_ROWS).astype(F32)
    n_blocks = -(-(2 * n) // MOE_ROWS) + N_EXPERTS
    ids = jnp.arange(n_blocks, dtype=jnp.int32)
    blk_e = jnp.minimum(jnp.searchsorted(blk_end, ids, side='right'), N_EXPERTS - 1).astype(jnp.int32)
    blk_valid = (ids < blk_end[-1]).astype(jnp.int32)
    start_l = jnp.zeros((1, LANES), F32).at[0, :N_EXPERTS].set(pad_start)
    dest = _rank(oh, route, start_l)[:, :2].astype(jnp.int32).reshape(-1)
    xb = _scatter(dest, h2, n_blocks * MOE_ROWS)
    yb = _experts(blk_e, blk_valid, xb, w1, w3, w2)
    return _combine(dest, route, yb)


def _final_kernel(x_ref, f_ref, g2_ref, gn_ref, y_ref):
    x = x_ref[...] + g2_ref[...] * f_ref[...]
    y_ref[...] = x * lax.rsqrt(jnp.mean(x * x, axis=-1, keepdims=True) + RMS_EPS) * gn_ref[...]


def _final(x1, f, g2, gn, nb, r):
    NB, R, D = x1.shape
    spec = pl.BlockSpec((nb, r, D), lambda i, j: (i, j, 0))
    return pl.pallas_call(
        _final_kernel,
        out_shape=jax.ShapeDtypeStruct((NB, R, D), F32),
        grid=(NB // nb, R // r),
        in_specs=[spec, spec, pl.BlockSpec((nb, 1, D), lambda i, j: (i, 0, 0)),
                  pl.BlockSpec((1, 1, D), lambda i, j: (0, 0, 0))],
        out_specs=spec,
        compiler_params=_cp(("arbitrary", "arbitrary")),
        name="final_norm",
    )(x1, f, g2, gn.reshape(1, 1, D))


def _page_specs(n_ops, block, page_of):
    def spec(p):
        return pl.BlockSpec(block, lambda s, j, pt: (pt[s, page_of(j, p)],) + (0,) * (len(block) - 1))
    return [spec(p) for p in range(n_ops)]


def _compress_sample_kernel(pt_ref, *refs, P, nj):
    del pt_ref
    pages = refs[:P]
    wa_ref, wb_ref, pea_ref, peb_ref, b1_ref, w2_ref, kc_ref, vc_ref, ha_ref, hb_ref = refs[P:]
    j = pl.program_id(1)
    x = jnp.concatenate([r[0] for r in pages], axis=0)
    rows = x.shape[0]
    sl = pl.ds(pl.multiple_of(j * rows, rows), rows)
    ha_ref[sl, :] = _dot((x + pea_ref[...]).astype(BF16), wa_ref[...])
    hb_ref[sl, :] = _dot((x + peb_ref[...]).astype(BF16), wb_ref[...])

    @pl.when(j == nj - 1)
    def _():
        _compress_tail(ha_ref[...], hb_ref[...], b1_ref[...], w2_ref[...], kc_ref, vc_ref)


def _compress_sample(cache, page_table, cw):
    wa, wb, pea, peb, b1e, w2bd = cw
    Bd, n_pages = page_table.shape
    sub_per_page = PAGE_SIZE // CMP_STRIDE
    n_sub = n_pages * sub_per_page
    kdim = CMP_STRIDE * 256
    x = cache.reshape(cache.shape[0], sub_per_page, kdim)
    P = min(16, n_pages)
    nj = n_pages // P
    G = NSA_KV_HEADS
    full = lambda a: pl.BlockSpec(a.shape, lambda s, j, pt: (0,) * a.ndim)
    out_sd = jax.ShapeDtypeStruct((G, Bd * n_sub, HEAD_DIM), BF16)
    out_spec = pl.BlockSpec((G, n_sub, HEAD_DIM), lambda s, j, pt: (0, s, 0))
    gs = pltpu.PrefetchScalarGridSpec(
        num_scalar_prefetch=1,
        grid=(Bd, nj),
        in_specs=_page_specs(P, (1, sub_per_page, kdim), lambda j, p: j * P + p)
        + [full(wa), full(wb), full(pea), full(peb), full(b1e), full(w2bd)],
        out_specs=[out_spec, out_spec],
        scratch_shapes=[pltpu.VMEM((n_sub, 256), F32), pltpu.VMEM((n_sub, 256), F32)],
    )
    return pl.pallas_call(
        functools.partial(_compress_sample_kernel, P=P, nj=nj),
        out_shape=[out_sd, out_sd],
        grid_spec=gs,
        compiler_params=_cp(("arbitrary", "arbitrary")),
        name="compress_sample",
    )(page_table, *([x] * P), wa, wb, pea, peb, b1e, w2bd)


def _masked_softmax(s, valid):
    s = jnp.where(valid, s, NEG)
    m = jnp.max(s, axis=-1, keepdims=True)
    e = jnp.where(valid, jnp.exp(s - m), 0.0)
    den = jnp.sum(e, axis=-1, keepdims=True)
    return e / jnp.where(den > 0, den, 1.0)


def _gate_col(gts, col):
    lane = lax.broadcasted_iota(jnp.int32, gts.shape, 1)
    return jnp.sum(jnp.where(lane == col, gts, 0.0), axis=-1, keepdims=True)


def _nsa_sample_local_kernel(q_ref, kc_ref, vc_ref, win_ref, gate_ref, bc_ref, bw_ref, ov_ref, ocw_ref, sel_ref,
                             *, n_sub, n_pb, k_pick, w_buf):
    R, G = NSA_REP, NSA_KV_HEADS
    S = q_ref.shape[1]
    rows = R * S
    hd = HEAD_DIM
    wk = win_ref.shape[1]
    win = win_ref[0]
    gts = gate_ref[...]
    for g in range(G):
        q2 = q_ref[g * R:(g + 1) * R].reshape(rows, hd).astype(BF16)
        s = _nt(q2, kc_ref[g]) + bc_ref[g * R:(g + 1) * R].reshape(rows, n_sub)
        c = lax.broadcasted_iota(jnp.int32, (rows, n_sub), 1)
        p_c = _masked_softmax(s, c < n_sub - 1)
        o_c = _dot(p_c.astype(BF16), vc_ref[g])
        imp = _exact_dot(jnp.sum(p_c.reshape(R, S, n_sub), axis=0), ov_ref[...])
        blk = lax.broadcasted_iota(jnp.int32, (S, LANES), 1)
        forced = (blk == 0) | (blk == n_pb - 1)
        imp = jnp.where(forced, FORCE_SCORE, jnp.where(blk < n_pb, imp, -jnp.inf))
        sel = jnp.where(blk < n_pb, _topk_mask(imp, k_pick), 0.0)
        sel_ref[0, g * rows:(g + 1) * rows, :] = jnp.concatenate([sel] * R, axis=0)
        k_w = win[:, g * hd:(g + 1) * hd].astype(BF16)
        v_w = win[:, (G + g) * hd:(G + g + 1) * hd].astype(BF16)
        s = _nt(q2, k_w) + bw_ref[g * R:(g + 1) * R].reshape(rows, wk)
        jj = lax.broadcasted_iota(jnp.int32, (rows, wk), 1)
        ii = lax.broadcasted_iota(jnp.int32, (rows, wk), 0) % S
        valid = ((jj < w_buf) & (jj > ii)) | ((jj >= w_buf) & (jj - w_buf <= ii))
        o_w = _dot(_masked_softmax(s, valid).astype(BF16), v_w)
        for r in range(R):
            hh = g * R + r
            sl = slice(r * S, (r + 1) * S)
            ocw_ref[hh] = _gate_col(gts, hh * 3) * o_c[sl] + _gate_col(gts, hh * 3 + 2) * o_w[sl]


def _nsa_sample_tables(rel_bias, past, S, n_sub, w_buf, wk):
    tb = _bias_by_distance(rel_bias)
    i = np.arange(S)[:, None]
    c = np.arange(n_sub)[None, :]
    idx_c = np.clip(past + i - (CMP_STRIDE * c + CMP_LEN - 1), 0, REL_MAX_DIST)
    j = np.arange(wk)[None, :]
    idx_w = np.where(j < w_buf, np.clip(w_buf + i - j, 0, REL_MAX_DIST), np.clip(i - (j - w_buf), 0, REL_MAX_DIST))
    return tb[:, idx_c], tb[:, idx_w], tb


def _nsa_sample_local(qa, kc, vc, win_all, gates, bias_c, bias_w, Bd, S, n_sub, n_pb, w_buf):
    G, R = NSA_KV_HEADS, NSA_REP
    wk = win_all.shape[1]
    ov = jnp.asarray(_overlap_np(n_sub), BF16)
    k_pick = min(N_SEL, n_pb + 1) - 1
    kern = functools.partial(_nsa_sample_local_kernel, n_sub=n_sub, n_pb=n_pb, k_pick=k_pick, w_buf=w_buf)
    hm_spec = pl.BlockSpec((NSA_HEADS, S, HEAD_DIM), lambda s: (0, s, 0))
    cmp_spec = pl.BlockSpec((G, n_sub, HEAD_DIM), lambda s: (0, s, 0))
    full = lambda a: pl.BlockSpec(a.shape, lambda s: (0,) * a.ndim)
    return pl.pallas_call(
        kern,
        out_shape=[jax.ShapeDtypeStruct((NSA_HEADS, Bd * S, HEAD_DIM), F32),
                   jax.ShapeDtypeStruct((Bd, G * R * S, LANES), F32)],
        grid=(Bd,),
        in_specs=[hm_spec, cmp_spec, cmp_spec, pl.BlockSpec((1, wk, 256), lambda s: (s, 0, 0)),
                  pl.BlockSpec((S, LANES), lambda s: (s, 0)), full(bias_c), full(bias_w), full(ov)],
        out_specs=[hm_spec, pl.BlockSpec((1, G * R * S, LANES), lambda s: (s, 0, 0))],
        compiler_params=_cp(("arbitrary",)),
        name="nsa_sample_local",
    )(qa, kc, vc, win_all, gates, bias_c, bias_w, ov)


def _nsa_sample_slc_kernel(pt_ref, *refs, P, nj):
    del pt_ref
    pages = refs[:P]
    (q_ref, sel_ref, ocw_ref, gate_ref, new_ref, blast_ref, bnew_ref, cb_ref, o_ref, m_ref, l_ref, acc_ref) = refs[P:]
    R, G = NSA_REP, NSA_KV_HEADS
    S = q_ref.shape[1]
    rows = R * S
    hd = HEAD_DIM
    j = pl.program_id(1)

    @pl.when(j == 0)
    def _():
        m_ref[...] = jnp.full_like(m_ref, NEG)
        l_ref[...] = jnp.zeros_like(l_ref)
        acc_ref[...] = jnp.zeros_like(acc_ref)

    def q_of(g):
        return q_ref[g * R:(g + 1) * R].reshape(rows, hd).astype(BF16)

    def update(g, kv, s_bias, valid):
        k = kv[:, g * hd:(g + 1) * hd].astype(BF16)
        v = kv[:, (G + g) * hd:(G + g + 1) * hd].astype(BF16)
        carry = _flash_step((m_ref[g], l_ref[g], acc_ref[g]), _nt(q_of(g), k) + s_bias, valid, v)
        m_ref[g], l_ref[g], acc_ref[g] = carry

    kv = jnp.concatenate([r[0] for r in pages], axis=0)
    nk = kv.shape[0]
    e_b = lax.broadcasted_iota(jnp.int32, (LANES, nk), 0)
    e_k = lax.broadcasted_iota(jnp.int32, (LANES, nk), 1) // SEL_BLOCK
    expand = jnp.where(e_b == j * (nk // SEL_BLOCK) + e_k, 1.0, 0.0).astype(BF16)
    for g in range(G):
        rs = slice(g * rows, (g + 1) * rows)
        bias = jnp.where(j == nj - 1, blast_ref[rs, :], cb_ref[rs, :])
        valid = _dot(sel_ref[0, rs, :].astype(BF16), expand) > 0.5
        update(g, kv, bias, valid)

    @pl.when(j == nj - 1)
    def _():
        new = jnp.concatenate([new_ref[...], jnp.zeros((PAGE_SIZE - S, new_ref.shape[1]), F32)], axis=0)
        kk = lax.broadcasted_iota(jnp.int32, (rows, PAGE_SIZE), 1)
        ii = lax.broadcasted_iota(jnp.int32, (rows, PAGE_SIZE), 0) % S
        gts = gate_ref[...]
        for g in range(G):
            rs = slice(g * rows, (g + 1) * rows)
            update(g, new, bnew_ref[rs, :], kk <= ii)
            o_s = _flash_out((m_ref[g], l_ref[g], acc_ref[g]))
            for r in range(R):
                hh = g * R + r
                o_ref[hh] = ocw_ref[hh] + _gate_col(gts, hh * 3 + 1) * o_s[r * S:(r + 1) * S]


def _nsa_sample_slc(cache, page_table, qa, sel, ocw, gates, kvs_new, tb, S, past):
    Bd, n_pages = page_table.shape
    G, R = NSA_KV_HEADS, NSA_REP
    P = min(16, n_pages)
    nj = n_pages // P
    nk = P * PAGE_SIZE
    x = cache.reshape(cache.shape[0], PAGE_SIZE, 256)
    i = np.arange(S)[:, None]
    k = np.arange(nk)[None, :]
    blast = tb[:, np.clip(past + i - (past - nk + k), 0, REL_MAX_DIST)].reshape(NSA_HEADS * S, nk)
    bnew = tb[:, np.clip(i - np.arange(PAGE_SIZE)[None, :], 0, REL_MAX_DIST)].reshape(NSA_HEADS * S, PAGE_SIZE)
    cbr = jnp.broadcast_to(tb[:, REL_MAX_DIST][:, None, None], (NSA_HEADS, S, 1)).reshape(NSA_HEADS * S, 1)
    hm_spec = pl.BlockSpec((NSA_HEADS, S, HEAD_DIM), lambda s, j, pt: (0, s, 0))
    full = lambda a: pl.BlockSpec(a.shape, lambda s, j, pt: (0,) * a.ndim)
    rows = R * S
    gs = pltpu.PrefetchScalarGridSpec(
        num_scalar_prefetch=1,
        grid=(Bd, nj),
        in_specs=_page_specs(P, (1, PAGE_SIZE, 256), lambda j, p: j * P + p)
        + [hm_spec, pl.BlockSpec((1, G * rows, LANES), lambda s, j, pt: (s, 0, 0)), hm_spec,
           pl.BlockSpec((S, LANES), lambda s, j, pt: (s, 0)), pl.BlockSpec((S, 256), lambda s, j, pt: (s, 0)),
           full(blast), full(bnew), full(cbr)],
        out_specs=hm_spec,
        scratch_shapes=[pltpu.VMEM((G, rows, 1), F32), pltpu.VMEM((G, rows, 1), F32),
                        pltpu.VMEM((G, rows, HEAD_DIM), F32)],
    )
    return pl.pallas_call(
        functools.partial(_nsa_sample_slc_kernel, P=P, nj=nj),
        out_shape=jax.ShapeDtypeStruct((NSA_HEADS, Bd * S, HEAD_DIM), F32),
        grid_spec=gs,
        compiler_params=_cp(("arbitrary", "arbitrary")),
        name="nsa_sample_slc",
    )(page_table, *([x] * P), qa, sel, ocw, gates, kvs_new, blast, bnew, cbr)


def _sb_sample_kernel(pt_ref, *refs, P, nj):
    del pt_ref
    pages = refs[:P]
    q_ref, new_ref, tri_ref, rep_ref, fold_ref, o_ref, suf_ref, acc_ref, qbd_ref = refs[P:]
    H = SB_HEADS
    S = q_ref.shape[1]
    rows = H * S
    width = H * HEAD_DIM
    j = pl.program_id(1)
    tri = tri_ref[...]
    r_io = lax.broadcasted_iota(jnp.int32, (rows, width), 0) // S
    c_io = lax.broadcasted_iota(jnp.int32, (rows, width), 1) // HEAD_DIM
    own = r_io == c_io

    @pl.when(j == 0)
    def _():
        q64 = q_ref[...].reshape(rows, HEAD_DIM).astype(BF16)
        qbd = jnp.where(own, _dot(q64, rep_ref[...]), 0.0).astype(BF16)
        qbd_ref[...] = qbd
        new = jnp.concatenate([new_ref[...], jnp.zeros((PAGE_SIZE - S, new_ref.shape[1]), F32)], axis=0)
        kk = lax.broadcasted_iota(jnp.int32, (rows, PAGE_SIZE), 1)
        ii = lax.broadcasted_iota(jnp.int32, (rows, PAGE_SIZE), 0) % S
        suf, acc = _sb_tile(qbd, new[:, :width].astype(BF16), new[:, width:].astype(BF16), tri,
                            jnp.zeros((rows, 1), F32), jnp.zeros((rows, width), F32), kk < ii)
        suf_ref[...] = suf
        acc_ref[...] = acc

    qbd = qbd_ref[...]
    suf, acc = suf_ref[...], acc_ref[...]
    for r in pages:
        page = r[0]
        suf, acc = _sb_tile(qbd, page[:, :width].astype(BF16), page[:, width:].astype(BF16), tri, suf, acc, None)
    suf_ref[...] = suf
    acc_ref[...] = acc

    @pl.when(j == nj - 1)
    def _():
        o = _exact_dot(jnp.where(own, acc, 0.0), fold_ref[...])
        o_ref[...] = o.reshape(H, S, HEAD_DIM)


def _sb_sample(cache, page_table, qb, kvsb_new, S):
    Bd, n_pages = page_table.shape
    H = SB_HEADS
    width = H * HEAD_DIM
    P = min(8, n_pages)
    nj = n_pages // P
    x = cache.reshape(cache.shape[0], PAGE_SIZE, 2 * width)
    tri = _suffix_matrix(PAGE_SIZE)
    rep = jnp.asarray(np.tile(np.eye(HEAD_DIM, dtype=np.float32), (1, H)), BF16)
    fold = jnp.asarray(np.tile(np.eye(HEAD_DIM, dtype=np.float32), (H, 1)), BF16)
    hm_spec = pl.BlockSpec((H, S, HEAD_DIM), lambda s, j, pt: (0, s, 0))
    full = lambda a: pl.BlockSpec(a.shape, lambda s, j, pt: (0,) * a.ndim)
    gs = pltpu.PrefetchScalarGridSpec(
        num_scalar_prefetch=1,
        grid=(Bd, nj),
        in_specs=_page_specs(P, (1, PAGE_SIZE, 2 * width), lambda j, p: n_pages - 1 - (j * P + p))
        + [hm_spec, pl.BlockSpec((S, 2 * width), lambda s, j, pt: (s, 0)), full(tri), full(rep), full(fold)],
        out_specs=hm_spec,
        scratch_shapes=[pltpu.VMEM((H * S, 1), F32), pltpu.VMEM((H * S, width), F32), pltpu.VMEM((H * S, width), BF16)],
    )
    return pl.pallas_call(
        functools.partial(_sb_sample_kernel, P=P, nj=nj),
        out_shape=jax.ShapeDtypeStruct((H, Bd * S, HEAD_DIM), F32),
        grid_spec=gs,
        compiler_params=_cp(("arbitrary", "arbitrary")),
        name="sb_sample",
    )(page_table, *([x] * P), qb, kvsb_new, tri, rep, fold)


def kernel(x_prompt, x_sample, c_prompt, c_sample, cache_sb, cache_cmp, cache_slc, cache_win, page_table, rel_bias, final_norm_g, ada_w, ada_b, norm1_g, norm2_g, w_in, cmp_pe, cmp_w1, cmp_b1, cmp_w2, out_norm_a, out_norm_b, w_out, router_group_w, router_group_b, router_expert_w, router_expert_b, expert_w1, expert_w3, expert_w2):
    assert ada_w.shape[0] == 1, "single-layer trunk"
    l = 0
    B, T, D = x_prompt.shape
    Bd, S, _ = x_sample.shape
    n_pages = page_table.shape[1]
    past = n_pages * PAGE_SIZE
    w_buf = cache_win.shape[2]
    assert S < CMP_STRIDE and w_buf == WINDOW and past >= WINDOW and 2 * n_pages <= LANES and T // SEL_BLOCK <= LANES
    G, hd = NSA_KV_HEADS, HEAD_DIM

    wi = w_in[l]
    n_gate = 3 * NSA_HEADS
    off_gate = NSA_HEADS * hd + 3 * 2 * G * hd
    w_perm = jnp.concatenate([wi[:, :off_gate], wi[:, off_gate + n_gate:], wi[:, off_gate:off_gate + n_gate],
                              jnp.zeros((D, _C_END - _C_GATE - n_gate), wi.dtype)], axis=1).astype(BF16)
    w_out_h = w_out[l].reshape(NSA_HEADS + SB_HEADS, hd, D).astype(BF16)
    wr = jnp.concatenate([router_group_w[l], router_expert_w[l],
                          jnp.zeros((D, LANES - N_GROUPS - N_EXPERTS), F32)], axis=1).astype(BF16)
    br = jnp.concatenate([router_group_b[l], router_expert_b[l],
                          jnp.zeros((LANES - N_GROUPS - N_EXPERTS,), F32)]).reshape(1, LANES)
    ga = out_norm_a[l].reshape(NSA_HEADS, 1, hd)
    gb = out_norm_b[l].reshape(SB_HEADS, 1, hd)
    n2g = norm2_g[l].reshape(1, 1, D)
    cw = _compress_weights(cmp_pe[l], cmp_w1[l], cmp_b1[l], cmp_w2[l])

    n_c = B + Bd
    c_all = jnp.concatenate([c_prompt, c_sample, jnp.zeros((-n_c % 8, D), F32)], axis=0)
    mod = _ada(c_all, ada_w[l], ada_b[l])
    mod_p = mod[:B].reshape(B, 6, 1, D)
    mod_s = mod[B:n_c].reshape(Bd, 6, 1, D)

    rp = min(256, T)
    (qa, qb, ks, vs, kw, vw, ksb, vsb, kvc, kvs, kvw, kvsb, gates) = _inproj(
        x_prompt, mod_p[:, 1], mod_p[:, 0], norm1_g[l], w_perm, 1, rp)
    kc, vc = _compress_prompt(kvc, B, T, cw)
    oa = _nsa_prompt(qa, kc, vc, ks, vs, kw, vw, gates, _nsa_prompt_tables(rel_bias), B, T)
    ob = _sb_prompt(qb, ksb, vsb, B, T)
    x1_p, h2_p, route_p, oh_p, cnt_p = _outproj(oa, ob, x_prompt, mod_p[:, 2], mod_p[:, 4], mod_p[:, 3],
                                                ga, gb, n2g, w_out_h, wr, br, 1, rp)
    st_p = (kvsb.reshape(1, B, T, 2, SB_HEADS, hd), kvc.reshape(1, B, T, 2, G, hd), kvs.reshape(1, B, T, 2, G, hd),
            kvw.reshape(B, T, 2, G, hd)[None, :, T - min(WINDOW, T):])

    ns = max(1, LANES // S)
    (qa_s, qb_s, _, _, _, _, _, _, kvc_s, kvs_s, kvw_s, kvsb_s, gates_s) = _inproj(
        x_sample, mod_s[:, 1], mod_s[:, 0], norm1_g[l], w_perm, ns, S)
    qa_s = qa_s.astype(F32)
    qb_s = qb_s.astype(F32)
    n_sub = past // CMP_STRIDE
    n_pb = past // SEL_BLOCK
    wk = -(-(w_buf + S) // LANES) * LANES
    win_new = jnp.concatenate([cache_win[l].reshape(Bd, w_buf, 2 * G * hd), kvw_s.reshape(Bd, S, 2 * G * hd)], axis=1)
    win_all = jnp.concatenate([win_new, jnp.zeros((Bd, wk - w_buf - S, 2 * G * hd), F32)], axis=1)
    kc_s, vc_s = _compress_sample(cache_cmp[l], page_table, cw)
    bias_c, bias_w, tb = _nsa_sample_tables(rel_bias, past, S, n_sub, w_buf, wk)
    ocw, sel = _nsa_sample_local(qa_s, kc_s, vc_s, win_all, gates_s, bias_c, bias_w, Bd, S, n_sub, n_pb, w_buf)
    oa_s = _nsa_sample_slc(cache_slc[l], page_table, qa_s, sel, ocw, gates_s, kvs_s, tb, S, past)
    ob_s = _sb_sample(cache_sb[l], page_table, qb_s, kvsb_s, S)
    x1_s, h2_s, route_s, oh_s, cnt_s = _outproj(oa_s, ob_s, x_sample, mod_s[:, 2], mod_s[:, 4], mod_s[:, 3],
                                                ga, gb, n2g, w_out_h, wr, br, ns, S)
    st_s = (kvsb_s.reshape(1, Bd, S, 2, SB_HEADS, hd), kvc_s.reshape(1, Bd, S, 2, G, hd),
            kvs_s.reshape(1, Bd, S, 2, G, hd), win_new[:, S:].reshape(1, Bd, w_buf, 2, G, hd))

    n_p = B * T
    f = _moe(jnp.concatenate([h2_p, h2_s], axis=0), jnp.concatenate([route_p, route_s], axis=0),
             jnp.concatenate([oh_p, oh_s], axis=0), cnt_p + cnt_s, expert_w1[l], expert_w3[l], expert_w2[l])
    y_p = _final(x1_p.reshape(B, T, D), f[:n_p].reshape(B, T, D), mod_p[:, 5], final_norm_g, 1, rp)
    y_s = _final(x1_s.reshape(Bd, S, D), f[n_p:].reshape(Bd, S, D), mod_s[:, 5], final_norm_g, ns, S)
    return (y_p, y_s) + st_p + st_s
```

```python
import functools
import math

import numpy as np
import jax
import jax.numpy as jnp
from jax import lax
from jax.experimental import pallas as pl
from jax.experimental.pallas import tpu as pltpu

F32 = jnp.float32
BF16 = jnp.bfloat16

HEAD_DIM = 64
NSA_KV_HEADS = 2
NSA_REP = 4
NSA_HEADS = NSA_KV_HEADS * NSA_REP
SB_HEADS = 8
CMP_STRIDE = 16
CMP_LEN = 32
CMP_HIDDEN = 64
SEL_BLOCK = 64
N_SEL = 16
WINDOW = 512
REL_BUCKETS = 32
REL_MAX_DIST = 128
N_GROUPS = 4
EXPERTS_PER_GROUP = 8
N_EXPERTS = N_GROUPS * EXPERTS_PER_GROUP
PAGE_SIZE = 128
RMS_EPS = 1e-6
FORCE_SCORE = 1e9
SCALE = HEAD_DIM ** -0.5

LOG2E = 1.4426950408889634
QSCALE = SCALE * LOG2E

LANES = 128
NEG = -1e30
VMEM_LIMIT = 56 * 1024 * 1024
MOE_ROWS = 256


def _cp(sem, vmem=VMEM_LIMIT):
    return pltpu.CompilerParams(dimension_semantics=sem, vmem_limit_bytes=vmem)


def _nt(a, b):
    return lax.dot_general(a, b, (((1,), (1,)), ((), ())), preferred_element_type=F32)


def _dot(a, b):
    return jnp.dot(a, b, preferred_element_type=F32)


def _split3(x):
    hi = x.astype(BF16)
    r1 = x - hi.astype(F32)
    mid = r1.astype(BF16)
    lo = (r1 - mid.astype(F32)).astype(BF16)
    return hi, mid, lo


def _exact_dot(x, m01):
    hi, mid, lo = _split3(x)
    return _dot(hi, m01) + _dot(mid, m01) + _dot(lo, m01)


def _dot2(x, m01):
    hi = x.astype(BF16)
    lo = (x - hi.astype(F32)).astype(BF16)
    return _dot(hi, m01) + _dot(lo, m01)


def _bucket_np(d):
    n = np.maximum(d, 0)
    exact = REL_BUCKETS // 2
    nf = np.maximum(n, 1).astype(np.float64)
    large = exact + (np.log(nf / exact) / math.log(REL_MAX_DIST / exact) * (REL_BUCKETS - exact)).astype(np.int64)
    return np.where(n < exact, n, np.minimum(large, REL_BUCKETS - 1)).astype(np.int32)


def _bias_by_distance(rel_bias):
    return rel_bias.astype(F32)[_bucket_np(np.arange(REL_MAX_DIST + 1))].T


def _ada_kernel(c_ref, w_ref, b_ref, o_ref):
    c = c_ref[...]
    a = c * (1.0 / (1.0 + jnp.exp(-c)))
    o_ref[...] = _dot(a.astype(BF16), w_ref[...].astype(BF16)) + b_ref[...]


def _ada(c, w, b):
    rows, d = c.shape
    n = w.shape[1]
    tn = 1536
    return pl.pallas_call(
        _ada_kernel,
        out_shape=jax.ShapeDtypeStruct((rows, n), F32),
        grid=(n // tn,),
        in_specs=[pl.BlockSpec((rows, d), lambda j: (0, 0)),
                  pl.BlockSpec((d, tn), lambda j: (0, j)),
                  pl.BlockSpec((1, tn), lambda j: (0, j))],
        out_specs=pl.BlockSpec((rows, tn), lambda j: (0, j)),
        compiler_params=_cp(("arbitrary",)),
        name="ada",
    )(c, w, b.reshape(1, n))


_C_QA, _C_KVC, _C_KVS, _C_KVW, _C_QB, _C_KSB, _C_VSB, _C_GATE, _C_END = 0, 512, 768, 1024, 1280, 1792, 2304, 2816, 2944


def _inproj_kernel(x_ref, sc_ref, sh_ref, g_ref, w_ref,
                   qa_ref, qb_ref, ks_ref, vs_ref, kw_ref, vw_ref, ksb_ref, vsb_ref,
                   kvc_ref, kvs_ref, kvw_ref, kvsb_ref, gate_ref):
    x = x_ref[...]
    nb, r, d = x.shape
    y = x * lax.rsqrt(jnp.mean(x * x, axis=-1, keepdims=True) + RMS_EPS) * g_ref[...]
    h = y * (1.0 + sc_ref[...]) + sh_ref[...]
    u = _dot(h.reshape(nb * r, d).astype(BF16), w_ref[...])
    kvc_ref[...] = u[:, _C_KVC:_C_KVS]
    kvs_ref[...] = u[:, _C_KVS:_C_KVW]
    kvw_ref[...] = u[:, _C_KVW:_C_QB]
    kvsb_ref[...] = u[:, _C_KSB:_C_GATE]
    gate_ref[...] = 1.0 / (1.0 + jnp.exp(-u[:, _C_GATE:_C_END]))
    hd = HEAD_DIM
    for hh in range(NSA_HEADS):
        qa_ref[hh] = (u[:, _C_QA + hh * hd:_C_QA + (hh + 1) * hd] * QSCALE).astype(BF16)
    for hh in range(SB_HEADS):
        qb_ref[hh] = (u[:, _C_QB + hh * hd:_C_QB + (hh + 1) * hd] * QSCALE).astype(BF16)
        ksb_ref[hh] = u[:, _C_KSB + hh * hd:_C_KSB + (hh + 1) * hd].astype(BF16)
        vsb_ref[hh] = u[:, _C_VSB + hh * hd:_C_VSB + (hh + 1) * hd].astype(BF16)
    lane = lax.broadcasted_iota(jnp.int32, (u.shape[0], LANES), 1)
    ones_col = jnp.where(lane == hd, 1.0, 0.0)

    def with_ones(pair, gg):
        v = pair if gg == 0 else pltpu.roll(pair, hd, 1)
        return jnp.where(lane < hd, v, ones_col).astype(BF16)

    g2 = NSA_KV_HEADS * hd
    for gg in range(NSA_KV_HEADS):
        ks_ref[gg] = u[:, _C_KVS + gg * hd:_C_KVS + (gg + 1) * hd].astype(BF16)
        kw_ref[gg] = u[:, _C_KVW + gg * hd:_C_KVW + (gg + 1) * hd].astype(BF16)
        vs_ref[gg] = with_ones(u[:, _C_KVS + g2:_C_KVS + 2 * g2], gg)
        vw_ref[gg] = with_ones(u[:, _C_KVW + g2:_C_KVW + 2 * g2], gg)


def _inproj(x3, sc, sh, g, w_perm, nb, r):
    NB, R, D = x3.shape
    n_tok = NB * R
    rows = nb * r
    grid = (NB // nb, R // r)
    hd = HEAD_DIM

    def row_map(i, j):
        return (i * (R // r) + j, 0)

    def hm_map(i, j):
        return (0, i * (R // r) + j, 0)

    def hm(nh, width=hd):
        return jax.ShapeDtypeStruct((nh, n_tok, width), BF16), pl.BlockSpec((nh, rows, width), hm_map)

    def flat(nc):
        return jax.ShapeDtypeStruct((n_tok, nc), F32), pl.BlockSpec((rows, nc), row_map)

    outs = [hm(NSA_HEADS), hm(SB_HEADS), hm(NSA_KV_HEADS), hm(NSA_KV_HEADS, LANES), hm(NSA_KV_HEADS),
            hm(NSA_KV_HEADS, LANES),
            hm(SB_HEADS), hm(SB_HEADS), flat(256), flat(256), flat(256), flat(1024), flat(LANES)]
    return pl.pallas_call(
        _inproj_kernel,
        out_shape=[o[0] for o in outs],
        grid=grid,
        in_specs=[pl.BlockSpec((nb, r, D), lambda i, j: (i, j, 0)),
                  pl.BlockSpec((nb, 1, D), lambda i, j: (i, 0, 0)),
                  pl.BlockSpec((nb, 1, D), lambda i, j: (i, 0, 0)),
                  pl.BlockSpec((1, 1, D), lambda i, j: (0, 0, 0)),
                  pl.BlockSpec(w_perm.shape, lambda i, j: (0, 0))],
        out_specs=[o[1] for o in outs],
        compiler_params=_cp(("arbitrary", "arbitrary")),
        name="inproj",
    )(x3, sc, sh, g.reshape(1, 1, D), w_perm)


def _compress_weights(pe, w1, b1, w2):
    G = NSA_KV_HEADS
    half = CMP_STRIDE
    w1r = w1.reshape(2, CMP_LEN, HEAD_DIM, CMP_HIDDEN)
    eye_s = jnp.eye(2, dtype=F32)
    eye_g = jnp.eye(G, dtype=F32)

    def expand(wpart):
        m = jnp.einsum('spde,ts,hg->pthdsge', wpart, eye_s, eye_g)
        return m.reshape(half * 2 * G * HEAD_DIM, 2 * G * CMP_HIDDEN).astype(BF16)

    wa = expand(w1r[:, :half])
    wb = expand(w1r[:, half:])
    pe_t = jnp.transpose(pe, (1, 0, 2))
    pe_e = jnp.broadcast_to(pe_t[:, :, None, :], (CMP_LEN, 2, G, HEAD_DIM))
    pea = pe_e[:half].reshape(1, -1).astype(F32)
    peb = pe_e[half:].reshape(1, -1).astype(F32)
    b1e = jnp.broadcast_to(b1[:, None, :], (2, G, CMP_HIDDEN)).reshape(1, -1).astype(F32)
    w2bd = jnp.einsum('sed,ts,hg->thesgd', w2, eye_s, eye_g).reshape(2 * G * CMP_HIDDEN, 2 * G * HEAD_DIM).astype(BF16)
    return wa, wb, pea, peb, b1e, w2bd


def _compress_tail(hid_a, hid_b, b1e, w2bd, kc_ref, vc_ref):
    n_sub = hid_a.shape[0]
    hid = hid_a + pltpu.roll(hid_b, n_sub - 1, 0) + b1e
    act = hid * (1.0 / (1.0 + jnp.exp(-hid)))
    out = _dot(act.astype(BF16), w2bd)
    hd = HEAD_DIM
    for gg in range(NSA_KV_HEADS):
        kc_ref[gg] = out[:, gg * hd:(gg + 1) * hd].astype(BF16)
        vc_ref[gg] = out[:, (NSA_KV_HEADS + gg) * hd:(NSA_KV_HEADS + gg + 1) * hd].astype(BF16)


def _compress_prompt_kernel(x_ref, wa_ref, wb_ref, pea_ref, peb_ref, b1_ref, w2_ref, kc_ref, vc_ref):
    x = x_ref[0]
    hid_a = _dot((x + pea_ref[...]).astype(BF16), wa_ref[...])
    hid_b = _dot((x + peb_ref[...]).astype(BF16), wb_ref[...])
    _compress_tail(hid_a, hid_b, b1_ref[...], w2_ref[...], kc_ref, vc_ref)


def _compress_prompt(kvc, B, T, cw):
    wa, wb, pea, peb, b1e, w2bd = cw
    n_sub = T // CMP_STRIDE
    kdim = CMP_STRIDE * 256
    x = kvc.reshape(B, n_sub, kdim)
    G = NSA_KV_HEADS
    full = lambda a: pl.BlockSpec(a.shape, lambda b: (0,) * a.ndim)
    out_sd = jax.ShapeDtypeStruct((G, B * n_sub, HEAD_DIM), BF16)
    out_spec = pl.BlockSpec((G, n_sub, HEAD_DIM), lambda b: (0, b, 0))
    return pl.pallas_call(
        _compress_prompt_kernel,
        out_shape=[out_sd, out_sd],
        grid=(B,),
        in_specs=[pl.BlockSpec((1, n_sub, kdim), lambda b: (b, 0, 0)),
                  full(wa), full(wb), full(pea), full(peb), full(b1e), full(w2bd)],
        out_specs=[out_spec, out_spec],
        compiler_params=_cp(("arbitrary",)),
        name="compress_prompt",
    )(x, wa, wb, pea, peb, b1e, w2bd)


def _topk_mask_t(imp, k):
    row = lax.broadcasted_iota(jnp.int32, imp.shape, 0)
    n = imp.shape[0]

    def body(_, carry):
        avail, sel = carry
        cur = jnp.where(avail > 0, imp, -jnp.inf)
        m = jnp.max(cur, axis=0, keepdims=True)
        cand = jnp.where((avail > 0) & (cur == m), row, n)
        idx = jnp.min(cand, axis=0, keepdims=True)
        hit = row == idx
        return jnp.where(hit, 0.0, avail), jnp.where(hit, 1.0, sel)

    _, sel = lax.fori_loop(0, k, body, (jnp.ones(imp.shape, F32), jnp.zeros(imp.shape, F32)))
    return sel


def _flash_step(carry, s, pv):
    m, l, acc = carry
    m_new = jnp.maximum(m, jnp.max(s, axis=-1, keepdims=True))
    alpha = jnp.exp2(m - m_new)
    p = jnp.exp2(s - m_new)
    return m_new, l * alpha + jnp.sum(p, axis=-1, keepdims=True), acc * alpha + pv(p.astype(BF16))


def _flash_init(rows):
    return (jnp.full((rows, 1), NEG, F32), jnp.zeros((rows, 1), F32), jnp.zeros((rows, HEAD_DIM), F32))


def _flash_out(carry):
    _, l, acc = carry
    return acc / jnp.where(l > 0, l, 1.0)


def _mask_add(keep):
    return jnp.where(keep, 0.0, NEG)


_BAND = 16
_BAND_LO = 9
_TQ = 128


def _nsa_prompt_tables(rel_bias):
    tb = _bias_by_distance(rel_bias) * LOG2E
    tb = tb - tb[:, REL_MAX_DIST:]
    qi = np.arange(_TQ)[:, None]
    idx_band = np.clip(qi - CMP_STRIDE * (np.arange(_BAND)[None, :] - _BAND_LO) - (CMP_LEN - 1), 0, REL_MAX_DIST)
    band = tb[:, idx_band]
    hi, mid, lo = _split3(band)
    band3 = jnp.concatenate([hi, mid, lo], axis=-1).reshape(NSA_HEADS * _TQ, 3 * _BAND)
    ki = np.arange(_TQ)[None, :]
    toep = jnp.stack([tb[:, np.clip(qi - ki, 0, REL_MAX_DIST)],
                      tb[:, np.clip(_TQ + qi - ki, 0, REL_MAX_DIST)]], axis=1)
    return band3, toep


def _overlap_np(n_sub):
    c = np.arange(n_sub)[:, None]
    j = np.arange(LANES)[None, :]
    c_start = c * CMP_STRIDE
    c_end = c_start + CMP_LEN - 1
    ov = (c_start < j * SEL_BLOCK + SEL_BLOCK) & (c_end >= j * SEL_BLOCK) & (c < n_sub - 1)
    return ov.astype(np.float32)


def _gate_col(gts, col):
    lane = lax.broadcasted_iota(jnp.int32, gts.shape, 1)
    return jnp.sum(jnp.where(lane == col, gts, 0.0), axis=-1, keepdims=True)


def _nsa_prompt_kernel(q_ref, kc_ref, vc_ref, ks_ref, vs_ref, kw_ref, vw_ref, gate_ref,
                       band_ref, toep_ref, ovt_ref, o_ref, *, n_sub, k_sel):
    R = NSA_REP
    tq = _TQ
    qb = pl.program_id(2)
    g = pl.program_id(1)
    q0 = qb * tq
    qs = [q_ref[r] for r in range(R)]

    lo = qb * (tq // CMP_STRIDE) - _BAND_LO
    r_io = lax.broadcasted_iota(jnp.int32, (3 * _BAND, n_sub), 0)
    c_io = lax.broadcasted_iota(jnp.int32, (3 * _BAND, n_sub), 1)
    shift = jnp.where(c_io == lo + r_io % _BAND, 1.0, 0.0).astype(BF16)
    c2 = lax.broadcasted_iota(jnp.int32, (tq, n_sub), 1)
    qpos = q0 + lax.broadcasted_iota(jnp.int32, (tq, n_sub), 0)
    valid = ((c2 * CMP_STRIDE + (CMP_LEN - 1)) <= qpos) & (c2 < n_sub - 1)
    kc = kc_ref[0]
    vc = vc_ref[0]
    o_c = []
    psum = jnp.zeros((tq, n_sub), F32)
    for r in range(R):
        s = _nt(qs[r], kc) + _dot(band_ref[r * tq:(r + 1) * tq, :], shift)
        s = jnp.where(valid, s, NEG)
        e = jnp.where(valid, jnp.exp2(s - jnp.max(s, axis=-1, keepdims=True)), 0.0)
        den = jnp.sum(e, axis=-1, keepdims=True)
        p = e / jnp.where(den > 0, den, 1.0)
        o_c.append(_dot(p.astype(BF16), vc))
        psum = psum + p

    ovt = ovt_ref[...]
    hi, mid, low = _split3(psum)
    imp = _nt(ovt, hi) + _nt(ovt, mid) + _nt(ovt, low)
    blk = lax.broadcasted_iota(jnp.int32, (LANES, tq), 0)
    cur = (q0 + lax.broadcasted_iota(jnp.int32, (LANES, tq), 1)) // SEL_BLOCK
    forced = (blk == 0) | (blk == cur) | (blk == cur - 1)
    imp = jnp.where(forced, FORCE_SCORE, jnp.where(blk <= cur, imp, -jnp.inf))
    sel = _topk_mask_t(imp, k_sel).T.astype(BF16)

    qi = lax.broadcasted_iota(jnp.int32, (tq, tq), 0)
    ki = lax.broadcasted_iota(jnp.int32, (tq, tq), 1)
    causal_add = _mask_add(ki <= qi)
    e_r = lax.broadcasted_iota(jnp.int32, (LANES, tq), 0)
    e_c = lax.broadcasted_iota(jnp.int32, (LANES, tq), 1) // SEL_BLOCK
    per_tile = tq // SEL_BLOCK

    def sel_add(kt):
        expand = jnp.where(e_r == kt * per_tile + e_c, 1.0, 0.0).astype(BF16)
        return (_dot(sel, expand) - 1.0) * (-NEG)

    def tile(ref, kt):
        return ref[0, pl.ds(pl.multiple_of(kt * tq, tq), tq), :]

    def off(j):
        return jnp.where(qb - j >= 0, 0.0, NEG)

    rows = R * tq
    q_all = q_ref[...].reshape(rows, HEAD_DIM)

    def tall(x):
        return jnp.concatenate([x] * R, axis=0)

    def toep(d):
        return toep_ref[:, d].reshape(rows, tq)

    def max_tile(mx, k, v, extra):
        return jnp.maximum(mx, _nt(q_all, k) + extra)

    def acc_tile(m):
        def fn(acc, k, v, extra):
            return acc + _dot(jnp.exp2(_nt(q_all, k) + extra - m).astype(BF16), v)
        return fn

    def finish(acc):
        return acc[:, :HEAD_DIM] / acc[:, HEAD_DIM:HEAD_DIM + 1]

    mx0 = jnp.full((rows, tq), NEG, F32)
    st0 = jnp.zeros((rows, LANES), F32)
    kt1 = jnp.maximum(qb - 1, 0)
    n_far = jnp.maximum(qb - 1, 0)

    n_chunk = n_far // 2
    kt2 = jnp.maximum(qb - 2, 0)
    odd_add = jnp.where(n_far % 2 == 1, 0.0, NEG)
    e_c2 = lax.broadcasted_iota(jnp.int32, (LANES, 2 * tq), 1) // SEL_BLOCK
    e_r2 = lax.broadcasted_iota(jnp.int32, (LANES, 2 * tq), 0)

    def chunk(ref, c):
        return ref[0, pl.ds(pl.multiple_of(c * (2 * tq), 2 * tq), 2 * tq), :]

    def chunk_add(c):
        expand = jnp.where(e_r2 == c * (2 * per_tile) + e_c2, 1.0, 0.0).astype(BF16)
        return tall((_dot(sel, expand) - 1.0) * (-NEG))

    def chunk_logits(c):
        return _nt(q_all, chunk(ks_ref, jnp.minimum(c, jnp.maximum(n_chunk - 1, 0))))

    def slc_near(fn, state):
        state = fn(state, tile(ks_ref, qb), tile(vs_ref, qb), toep(0) + tall(sel_add(qb) + causal_add))
        state = fn(state, tile(ks_ref, kt1), tile(vs_ref, kt1), toep(1) + tall(sel_add(kt1) + off(1)))
        return fn(state, tile(ks_ref, kt2), tile(vs_ref, kt2), tall(sel_add(kt2)) + odd_add)

    def p1_body(c, carry):
        mx, s = carry
        add = chunk_add(c)
        s_next = chunk_logits(c + 1)
        s = s + add
        return jnp.maximum(mx, jnp.maximum(s[:, :tq], s[:, tq:])), s_next

    mx, _ = lax.fori_loop(0, n_chunk, p1_body, (slc_near(max_tile, mx0), chunk_logits(0)))
    m_s = jnp.max(mx, axis=-1, keepdims=True)

    def p2_body(c, carry):
        acc, s = carry
        add = chunk_add(c)
        s_next = chunk_logits(c + 1)
        return acc + _dot(jnp.exp2(s + add - m_s).astype(BF16), chunk(vs_ref, c)), s_next

    acc, _ = lax.fori_loop(0, n_chunk, p2_body, (slc_near(acc_tile(m_s), st0), chunk_logits(0)))
    o_s = finish(acc)

    n_back = WINDOW // tq

    def win_sweep(fn, state):
        state = fn(state, tile(kw_ref, qb), tile(vw_ref, qb), toep(0) + tall(causal_add))
        state = fn(state, tile(kw_ref, kt1), tile(vw_ref, kt1), toep(1) + off(1))
        for j in range(2, n_back + 1):
            kt = jnp.maximum(qb - j, 0)
            add = (tall(_mask_add(ki > qi)) + off(j)) if j == n_back else off(j)
            state = fn(state, tile(kw_ref, kt), tile(vw_ref, kt), add)
        return state

    m_w = jnp.max(win_sweep(max_tile, mx0), axis=-1, keepdims=True)
    o_w = finish(win_sweep(acc_tile(m_w), st0))

    gts = gate_ref[...]
    for r in range(R):
        hh = g * R + r
        sl = slice(r * tq, (r + 1) * tq)
        o_ref[r] = (_gate_col(gts, hh * 3) * o_c[r] + _gate_col(gts, hh * 3 + 1) * o_s[sl]
                    + _gate_col(gts, hh * 3 + 2) * o_w[sl])


def _nsa_prompt(qa, kc, vc, ks, vs, kw, vw, gates, tables, B, T):
    band3, toep = tables
    G, R = NSA_KV_HEADS, NSA_REP
    n_sub = T // CMP_STRIDE
    nq = T // _TQ
    ovt = jnp.asarray(_overlap_np(n_sub).T, BF16)
    k_sel = min(N_SEL, -(-T // SEL_BLOCK))
    kern = functools.partial(_nsa_prompt_kernel, n_sub=n_sub, k_sel=k_sel)
    k_spec = pl.BlockSpec((1, T, HEAD_DIM), lambda b, g, i: (g, b, 0))
    v_spec = pl.BlockSpec((1, T, LANES), lambda b, g, i: (g, b, 0))
    cmp_spec = pl.BlockSpec((1, n_sub, HEAD_DIM), lambda b, g, i: (g, b, 0))
    return pl.pallas_call(
        kern,
        out_shape=jax.ShapeDtypeStruct((NSA_HEADS, B * T, HEAD_DIM), F32),
        grid=(B, G, nq),
        in_specs=[pl.BlockSpec((R, _TQ, HEAD_DIM), lambda b, g, i: (g, b * nq + i, 0)),
                  cmp_spec, cmp_spec, k_spec, v_spec, k_spec, v_spec,
                  pl.BlockSpec((_TQ, LANES), lambda b, g, i: (b * nq + i, 0)),
                  pl.BlockSpec((R * _TQ, 3 * _BAND), lambda b, g, i: (g, 0)),
                  pl.BlockSpec((R, 2, _TQ, _TQ), lambda b, g, i: (g, 0, 0, 0)),
                  pl.BlockSpec(ovt.shape, lambda b, g, i: (0, 0))],
        out_specs=pl.BlockSpec((R, _TQ, HEAD_DIM), lambda b, g, i: (g, b * nq + i, 0)),
        compiler_params=_cp(("arbitrary", "arbitrary", "arbitrary")),
        name="nsa_prompt",
    )(qa, kc, vc, ks, vs, kw, vw, gates, band3, toep, ovt)


_SB_T = 256


def _sb_logs(z):
    lsz = jnp.minimum(z, 0.0) - jnp.log(1.0 + jnp.exp2(-jnp.abs(z))) * LOG2E
    return lsz, lsz - z


def _sb_weights(z, tri, suf, mask):
    lsz, lk = _sb_logs(z)
    if mask is not None:
        lk = jnp.where(mask, lk, 0.0)
    att = jnp.exp2(lsz + _dot2(lk, tri) + suf)
    if mask is not None:
        att = jnp.where(mask, att, 0.0)
    return att.astype(BF16), suf + jnp.sum(lk, axis=-1, keepdims=True)


_SB_HP = 4


def _sb_prompt_kernel(q_ref, k_ref, v_ref, tri_ref, o_ref):
    t = _SB_T
    qb = pl.program_id(2)
    tri = tri_ref[...]
    qi = lax.broadcasted_iota(jnp.int32, (t, t), 0)
    ki = lax.broadcasted_iota(jnp.int32, (t, t), 1)
    qs = [q_ref[h] for h in range(_SB_HP)]

    def steps(kt, carry, mask):
        sl = pl.ds(pl.multiple_of(kt * t, t), t)
        hs = range(_SB_HP)
        logs = [_sb_logs(_nt(qs[h], k_ref[h, sl, :])) for h in hs]
        lks = [lk if mask is None else jnp.where(mask, lk, 0.0) for _, lk in logs]
        css = [_dot2(lk, tri) for lk in lks]
        atts = [jnp.exp2(logs[h][0] + css[h] + carry[h][0]) for h in hs]
        if mask is not None:
            atts = [jnp.where(mask, a, 0.0) for a in atts]
        accs = [carry[h][1] + _dot(atts[h].astype(BF16), v_ref[h, sl, :]) for h in hs]
        return tuple((carry[h][0] + jnp.sum(lks[h], axis=-1, keepdims=True), accs[h]) for h in hs)

    init = (jnp.zeros((t, 1), F32), jnp.zeros((t, HEAD_DIM), F32))
    carry = steps(qb, (init,) * _SB_HP, ki < qi)
    carry = lax.fori_loop(0, qb, lambda i, c: steps(qb - 1 - i, c, None), carry)
    for h in range(_SB_HP):
        o_ref[h] = carry[h][1]


def _suffix_matrix(t):
    i = np.arange(t)
    return jnp.asarray((i[:, None] > i[None, :]).astype(np.float32), BF16)


def _sb_prompt(qb, ksb, vsb, B, T):
    t = min(_SB_T, T)
    nq = T // t
    kv_spec = pl.BlockSpec((_SB_HP, T, HEAD_DIM), lambda b, h, i: (h, b, 0))
    q_spec = pl.BlockSpec((_SB_HP, t, HEAD_DIM), lambda b, h, i: (h, b * nq + i, 0))
    tri = _suffix_matrix(t)
    return pl.pallas_call(
        _sb_prompt_kernel,
        out_shape=jax.ShapeDtypeStruct((SB_HEADS, B * T, HEAD_DIM), F32),
        grid=(B, SB_HEADS // _SB_HP, nq),
        in_specs=[q_spec, kv_spec, kv_spec, pl.BlockSpec(tri.shape, lambda b, h, i: (0, 0))],
        out_specs=q_spec,
        compiler_params=_cp(("arbitrary", "arbitrary", "arbitrary")),
        name="sb_prompt",
    )(qb, ksb, vsb, tri)


def _outproj_kernel(oa_ref, ob_ref, x_ref, g1_ref, sc_ref, sh_ref, ga_ref, gb_ref, n2_ref, w_ref, wr_ref, br_ref,
                    x1_ref, h2_ref, route_ref, oh_ref, cnt_ref):
    nb, r, d = x_ref.shape
    rows = nb * r

    def normed(o_ref, g_ref):
        o = o_ref[...]
        ssq = jnp.sum(jnp.sum(o * o, axis=-1, keepdims=True), axis=0)
        rs = lax.rsqrt(ssq / (o.shape[0] * HEAD_DIM) + RMS_EPS)
        return [(o[hh] * rs * g_ref[hh]).astype(BF16) for hh in range(o.shape[0])]

    parts = normed(oa_ref, ga_ref) + normed(ob_ref, gb_ref)
    o = _dot(parts[0], w_ref[0])
    for hh in range(1, len(parts)):
        o = o + _dot(parts[hh], w_ref[hh])
    x1 = x_ref[...] + g1_ref[...] * o.reshape(nb, r, d)
    y = x1 * lax.rsqrt(jnp.mean(x1 * x1, axis=-1, keepdims=True) + RMS_EPS) * n2_ref[...]
    h2 = (y * (1.0 + sc_ref[...]) + sh_ref[...]).reshape(rows, d)
    x1_ref[...] = x1.reshape(rows, d)
    h2_ref[...] = h2

    logits = _dot(h2.astype(BF16), wr_ref[...]) + br_ref[...]
    lane = lax.broadcasted_iota(jnp.int32, (rows, LANES), 1)
    is_g = lane < N_GROUPS
    lg = jnp.where(is_g, logits, -jnp.inf)
    eg = jnp.where(is_g, jnp.exp(lg - jnp.max(lg, axis=-1, keepdims=True)), 0.0)
    pg = eg / jnp.sum(eg, axis=-1, keepdims=True)
    grp_w = jnp.max(pg, axis=-1, keepdims=True)
    grp = jnp.min(jnp.where(is_g & (pg == grp_w), lane, LANES), axis=-1, keepdims=True)
    base = N_GROUPS + grp * EXPERTS_PER_GROUP
    in_grp = (lane >= base) & (lane < base + EXPERTS_PER_GROUP)
    l1 = jnp.where(in_grp, logits, -jnp.inf)
    v1 = jnp.max(l1, axis=-1, keepdims=True)
    i1 = jnp.min(jnp.where(in_grp & (l1 == v1), lane, LANES), axis=-1, keepdims=True)
    l2 = jnp.where(in_grp & (lane != i1), logits, -jnp.inf)
    v2 = jnp.max(l2, axis=-1, keepdims=True)
    i2 = jnp.min(jnp.where(in_grp & (lane != i1) & (l2 == v2), lane, LANES), axis=-1, keepdims=True)
    e2 = jnp.exp(v2 - v1)
    w1 = grp_w * (1.0 / (1.0 + e2))
    w2 = grp_w * (e2 / (1.0 + e2))
    e1 = (i1 - N_GROUPS).astype(F32)
    e2i = (i2 - N_GROUPS).astype(F32)
    route_ref[...] = jnp.where(lane == 0, e1, jnp.where(lane == 1, e2i, jnp.where(lane == 2, w1, jnp.where(lane == 3, w2, 0.0))))
    oh = jnp.where((lane == i1 - N_GROUPS) | (lane == i2 - N_GROUPS), 1.0, 0.0)
    oh_ref[...] = oh.astype(BF16)

    @pl.when((pl.program_id(0) == 0) & (pl.program_id(1) == 0))
    def _():
        cnt_ref[...] = jnp.zeros_like(cnt_ref)

    cnt_ref[...] += jnp.sum(oh, axis=0, keepdims=True)


def _outproj(oa, ob, x3, g1, sc2, sh2, ga, gb, n2g, w_out_h, wr, br, nb, r):
    NB, R, D = x3.shape
    n_tok = NB * R
    rows = nb * r
    grid = (NB // nb, R // r)
    row_map = lambda i, j: (i * (R // r) + j, 0)
    hm_map = lambda i, j: (0, i * (R // r) + j, 0)
    mod_spec = pl.BlockSpec((nb, 1, D), lambda i, j: (i, 0, 0))
    full = lambda a: pl.BlockSpec(a.shape, lambda i, j: (0,) * a.ndim)
    return pl.pallas_call(
        _outproj_kernel,
        out_shape=[jax.ShapeDtypeStruct((n_tok, D), F32), jax.ShapeDtypeStruct((n_tok, D), F32),
                   jax.ShapeDtypeStruct((n_tok, LANES), F32), jax.ShapeDtypeStruct((n_tok, LANES), BF16),
                   jax.ShapeDtypeStruct((1, LANES), F32)],
        grid=grid,
        in_specs=[pl.BlockSpec((NSA_HEADS, rows, HEAD_DIM), hm_map), pl.BlockSpec((SB_HEADS, rows, HEAD_DIM), hm_map),
                  pl.BlockSpec((nb, r, D), lambda i, j: (i, j, 0)), mod_spec, mod_spec, mod_spec,
                  full(ga), full(gb), full(n2g), full(w_out_h), full(wr), full(br)],
        out_specs=[pl.BlockSpec((rows, D), row_map), pl.BlockSpec((rows, D), row_map),
                   pl.BlockSpec((rows, LANES), row_map), pl.BlockSpec((rows, LANES), row_map),
                   pl.BlockSpec((1, LANES), lambda i, j: (0, 0))],
        compiler_params=_cp(("arbitrary", "arbitrary")),
        name="outproj_router",
    )(oa, ob, x3, g1, sc2, sh2, ga, gb, n2g, w_out_h, wr, br)


_RANK_T = 512


def _rank_kernel(oh_ref, route_ref, start_ref, lower_ref, dest_ref, run_ref):
    @pl.when(pl.program_id(0) == 0)
    def _():
        run_ref[...] = jnp.zeros_like(run_ref)

    oh = oh_ref[...]
    pos = _dot(lower_ref[...], oh) + run_ref[...] + start_ref[...]
    route = route_ref[...]
    lane = lax.broadcasted_iota(jnp.int32, oh.shape, 1)

    def pick(col):
        e = jnp.sum(jnp.where(lane == col, route, 0.0), axis=-1, keepdims=True).astype(jnp.int32)
        return jnp.sum(jnp.where(lane == e, pos, 0.0), axis=-1, keepdims=True)

    dest_ref[...] = jnp.where(lane == 0, pick(0), jnp.where(lane == 1, pick(1), 0.0))
    run_ref[...] += jnp.sum(oh.astype(F32), axis=0, keepdims=True)


def _rank(oh, route, pad_start):
    n = oh.shape[0]
    t = _RANK_T
    i = np.arange(t)
    lower = jnp.asarray((i[None, :] < i[:, None]).astype(np.float32), BF16)
    return pl.pallas_call(
        _rank_kernel,
        out_shape=jax.ShapeDtypeStruct((n, LANES), F32),
        grid=(n // t,),
        in_specs=[pl.BlockSpec((t, LANES), lambda i: (i, 0)), pl.BlockSpec((t, LANES), lambda i: (i, 0)),
                  pl.BlockSpec((1, LANES), lambda i: (0, 0)), pl.BlockSpec((t, t), lambda i: (0, 0))],
        out_specs=pl.BlockSpec((t, LANES), lambda i: (i, 0)),
        scratch_shapes=[pltpu.VMEM((1, LANES), F32)],
        compiler_params=_cp(("arbitrary",)),
        name="moe_rank",
    )(oh, route, pad_start, lower)


_SCAT_T = 512


def _scatter_kernel(dest_ref, h_ref, xb_in_ref, xb_ref, sem):
    del xb_in_ref

    def row_copy(t, d):
        return pltpu.make_async_copy(h_ref.at[pl.ds(t, 1)], xb_ref.at[pl.ds(d, 1)], sem)

    def issue(t, c):
        row_copy(t, dest_ref[2 * t]).start()
        row_copy(t, dest_ref[2 * t + 1]).start()
        return c

    lax.fori_loop(0, _SCAT_T, issue, 0)

    def drain(t, c):
        row_copy(0, 0).wait()
        return c

    lax.fori_loop(0, 2 * _SCAT_T, drain, 0)


def _scatter(dest_flat, h2, n_rows):
    n, d = h2.shape
    xb0 = jnp.zeros((n_rows, d), h2.dtype)
    return pl.pallas_call(
        _scatter_kernel,
        out_shape=jax.ShapeDtypeStruct((n_rows, d), h2.dtype),
        grid=(n // _SCAT_T,),
        in_specs=[pl.BlockSpec((2 * _SCAT_T,), lambda i: (i,), memory_space=pltpu.SMEM),
                  pl.BlockSpec((_SCAT_T, d), lambda i: (i, 0)), pl.BlockSpec(memory_space=pl.ANY)],
        out_specs=pl.BlockSpec(memory_space=pl.ANY),
        scratch_shapes=[pltpu.SemaphoreType.DMA(())],
        input_output_aliases={2: 0},
        compiler_params=pltpu.CompilerParams(dimension_semantics=("arbitrary",), has_side_effects=True),
        name="moe_scatter",
    )(dest_flat, h2, xb0)


def _expert_kernel(be_ref, bv_ref, x_ref, w1_ref, w3_ref, w2_ref, y_ref):
    i = pl.program_id(0)

    @pl.when(bv_ref[i] > 0)
    def _():
        x = x_ref[...].astype(BF16)
        a = _dot(x, w1_ref[0].astype(BF16))
        b = _dot(x, w3_ref[0].astype(BF16))
        hmid = a * (1.0 / (1.0 + jnp.exp(-a))) * b
        y_ref[...] = _dot(hmid.astype(BF16), w2_ref[0].astype(BF16))

    @pl.when(bv_ref[i] == 0)
    def _():
        y_ref[...] = jnp.zeros_like(y_ref)


def _experts(blk_e, blk_valid, xb, w1, w3, w2):
    n_rows, d = xb.shape
    ff = w1.shape[2]
    n_blocks = n_rows // MOE_ROWS
    gs = pltpu.PrefetchScalarGridSpec(
        num_scalar_prefetch=2,
        grid=(n_blocks,),
        in_specs=[pl.BlockSpec((MOE_ROWS, d), lambda i, be, bv: (i, 0)),
                  pl.BlockSpec((1, d, ff), lambda i, be, bv: (be[i], 0, 0)),
                  pl.BlockSpec((1, d, ff), lambda i, be, bv: (be[i], 0, 0)),
                  pl.BlockSpec((1, ff, d), lambda i, be, bv: (be[i], 0, 0))],
        out_specs=pl.BlockSpec((MOE_ROWS, d), lambda i, be, bv: (i, 0)),
    )
    return pl.pallas_call(
        _expert_kernel,
        out_shape=jax.ShapeDtypeStruct((n_rows, d), F32),
        grid_spec=gs,
        compiler_params=_cp(("arbitrary",)),
        name="moe_experts",
    )(blk_e, blk_valid, xb, w1, w3, w2)


def _combine_kernel(dest_ref, route_ref, yb_ref, f_ref, buf0, buf1, sem):
    def row_copy(d, buf, t):
        return pltpu.make_async_copy(yb_ref.at[pl.ds(d, 1)], buf.at[pl.ds(t, 1)], sem)

    def issue(t, c):
        row_copy(dest_ref[2 * t], buf0, t).start()
        row_copy(dest_ref[2 * t + 1], buf1, t).start()
        return c

    lax.fori_loop(0, _SCAT_T, issue, 0)

    def drain(t, c):
        row_copy(0, buf0, 0).wait()
        return c

    lax.fori_loop(0, 2 * _SCAT_T, drain, 0)
    route = route_ref[...]
    lane = lax.broadcasted_iota(jnp.int32, route.shape, 1)
    w1 = jnp.sum(jnp.where(lane == 2, route, 0.0), axis=-1, keepdims=True)
    w2 = jnp.sum(jnp.where(lane == 3, route, 0.0), axis=-1, keepdims=True)
    f_ref[...] = w1 * buf0[...] + w2 * buf1[...]


def _combine(dest_flat, route, yb):
    n = route.shape[0]
    d = yb.shape[1]
    return pl.pallas_call(
        _combine_kernel,
        out_shape=jax.ShapeDtypeStruct((n, d), F32),
        grid=(n // _SCAT_T,),
        in_specs=[pl.BlockSpec((2 * _SCAT_T,), lambda i: (i,), memory_space=pltpu.SMEM),
                  pl.BlockSpec((_SCAT_T, LANES), lambda i: (i, 0)),
                  pl.BlockSpec(memory_space=pl.ANY)],
        out_specs=pl.BlockSpec((_SCAT_T, d), lambda i: (i, 0)),
        scratch_shapes=[pltpu.VMEM((_SCAT_T, d), F32), pltpu.VMEM((_SCAT_T, d), F32), pltpu.SemaphoreType.DMA(())],
        compiler_params=_cp(("arbitrary",)),
        name="moe_combine",
    )(dest_flat, route, yb)


def _moe(h2, route, oh, counts, w1, w3, w2):
    n = h2.shape[0]
    cnt = counts[0, :N_EXPERTS].astype(jnp.int32)
    nblk = (cnt + MOE_ROWS - 1) // MOE_ROWS
    blk_end = jnp.cumsum(nblk)
    pad_start = ((blk_end - nblk) * MOE_ROWS).astype(F32)
    n_blocks = -(-(2 * n) // MOE_ROWS) + N_EXPERTS
    ids = jnp.arange(n_blocks, dtype=jnp.int32)
    blk_e = jnp.minimum(jnp.sum((ids[:, None] >= blk_end[None, :]).astype(jnp.int32), axis=1), N_EXPERTS - 1)
    blk_valid = (ids < blk_end[-1]).astype(jnp.int32)
    start_l = jnp.zeros((1, LANES), F32).at[0, :N_EXPERTS].set(pad_start)
    dest = _rank(oh, route, start_l)[:, :2].astype(jnp.int32).reshape(-1)
    xb = _scatter(dest, h2, n_blocks * MOE_ROWS)
    yb = _experts(blk_e, blk_valid, xb, w1, w3, w2)
    return _combine(dest, route, yb)


def _final_kernel(x_ref, f_ref, g2_ref, gn_ref, y_ref):
    x = x_ref[...] + g2_ref[...] * f_ref[...]
    y_ref[...] = x * lax.rsqrt(jnp.mean(x * x, axis=-1, keepdims=True) + RMS_EPS) * gn_ref[...]


def _final(x1, f, g2, gn, nb, r):
    NB, R, D = x1.shape
    spec = pl.BlockSpec((nb, r, D), lambda i, j: (i, j, 0))
    return pl.pallas_call(
        _final_kernel,
        out_shape=jax.ShapeDtypeStruct((NB, R, D), F32),
        grid=(NB // nb, R // r),
        in_specs=[spec, spec, pl.BlockSpec((nb, 1, D), lambda i, j: (i, 0, 0)),
                  pl.BlockSpec((1, 1, D), lambda i, j: (0, 0, 0))],
        out_specs=spec,
        compiler_params=_cp(("arbitrary", "arbitrary")),
        name="final_norm",
    )(x1, f, g2, gn.reshape(1, 1, D))


def _page_specs(n_ops, block, page_of):
    def spec(p):
        return pl.BlockSpec(block, lambda s, j, pt: (pt[s, page_of(j, p)],) + (0,) * (len(block) - 1))
    return [spec(p) for p in range(n_ops)]


def _compress_sample_kernel(pt_ref, *refs, P, nj):
    del pt_ref
    pages = refs[:P]
    wa_ref, wb_ref, pea_ref, peb_ref, b1_ref, w2_ref, kc_ref, vc_ref, ha_ref, hb_ref = refs[P:]
    j = pl.program_id(1)
    x = jnp.concatenate([r[0] for r in pages], axis=0)
    rows = x.shape[0]
    sl = pl.ds(pl.multiple_of(j * rows, rows), rows)
    ha_ref[sl, :] = _dot((x + pea_ref[...]).astype(BF16), wa_ref[...])
    hb_ref[sl, :] = _dot((x + peb_ref[...]).astype(BF16), wb_ref[...])

    @pl.when(j == nj - 1)
    def _():
        _compress_tail(ha_ref[...], hb_ref[...], b1_ref[...], w2_ref[...], kc_ref, vc_ref)


def _compress_sample(cache, page_table, cw):
    wa, wb, pea, peb, b1e, w2bd = cw
    Bd, n_pages = page_table.shape
    sub_per_page = PAGE_SIZE // CMP_STRIDE
    n_sub = n_pages * sub_per_page
    kdim = CMP_STRIDE * 256
    x = cache.reshape(cache.shape[0], sub_per_page, kdim)
    P = min(16, n_pages)
    nj = n_pages // P
    G = NSA_KV_HEADS
    full = lambda a: pl.BlockSpec(a.shape, lambda s, j, pt: (0,) * a.ndim)
    out_sd = jax.ShapeDtypeStruct((G, Bd * n_sub, HEAD_DIM), BF16)
    out_spec = pl.BlockSpec((G, n_sub, HEAD_DIM), lambda s, j, pt: (0, s, 0))
    gs = pltpu.PrefetchScalarGridSpec(
        num_scalar_prefetch=1,
        grid=(Bd, nj),
        in_specs=_page_specs(P, (1, sub_per_page, kdim), lambda j, p: j * P + p)
        + [full(wa), full(wb), full(pea), full(peb), full(b1e), full(w2bd)],
        out_specs=[out_spec, out_spec],
        scratch_shapes=[pltpu.VMEM((n_sub, 256), F32), pltpu.VMEM((n_sub, 256), F32)],
    )
    return pl.pallas_call(
        functools.partial(_compress_sample_kernel, P=P, nj=nj),
        out_shape=[out_sd, out_sd],
        grid_spec=gs,
        compiler_params=_cp(("arbitrary", "arbitrary")),
        name="compress_sample",
    )(page_table, *([x] * P), wa, wb, pea, peb, b1e, w2bd)


def _masked_softmax(s, valid):
    s = jnp.where(valid, s, NEG)
    m = jnp.max(s, axis=-1, keepdims=True)
    e = jnp.where(valid, jnp.exp2(s - m), 0.0)
    den = jnp.sum(e, axis=-1, keepdims=True)
    return e / jnp.where(den > 0, den, 1.0)


def _pad_rows(x, n):
    return jnp.concatenate([x, jnp.zeros((n - x.shape[0], x.shape[1]), x.dtype)], axis=0)


def _nsa_sample_local_kernel(q_ref, kc_ref, vc_ref, win_ref, new_ref, gate_ref, bc_ref, bw_ref, ovt_ref, ocw_ref, sel_ref,
                             *, n_sub, n_pb, k_pick):
    R, G = NSA_REP, NSA_KV_HEADS
    S = q_ref.shape[1]
    rows = R * S
    hd = HEAD_DIM
    w_buf = win_ref.shape[2]
    win = win_ref[0]
    new = _pad_rows(new_ref[...], LANES)
    wk = w_buf + LANES
    gts = gate_ref[...]
    q2s, o_cs, psums = [], [], []
    for g in range(G):
        q2 = q_ref[g * R:(g + 1) * R].reshape(rows, hd).astype(BF16)
        s = _nt(q2, kc_ref[g]) + bc_ref[g * R:(g + 1) * R].reshape(rows, n_sub)
        c = lax.broadcasted_iota(jnp.int32, (rows, n_sub), 1)
        p_c = _masked_softmax(s, c < n_sub - 1)
        q2s.append(q2)
        o_cs.append(_dot(p_c.astype(BF16), vc_ref[g]))
        psums.append(jnp.sum(p_c.reshape(R, S, n_sub), axis=0))

    ovt = ovt_ref[...]
    hi, mid, low = _split3(_pad_rows(jnp.concatenate(psums, axis=0), LANES))
    imp = _nt(ovt, hi) + _nt(ovt, mid) + _nt(ovt, low)
    blk = lax.broadcasted_iota(jnp.int32, (LANES, LANES), 0)
    forced = (blk == 0) | (blk == n_pb - 1)
    imp = jnp.where(forced, FORCE_SCORE, jnp.where(blk < n_pb, imp, -jnp.inf))
    sel_t = jnp.where(blk < n_pb, _topk_mask_t(imp, k_pick), 0.0).astype(BF16)
    row_i = lax.broadcasted_iota(jnp.int32, (G * rows, LANES), 0)
    col_i = lax.broadcasted_iota(jnp.int32, (G * rows, LANES), 1)
    pick = jnp.where(col_i == (row_i // rows) * S + row_i % S, 1.0, 0.0).astype(BF16)
    sel_ref[0] = _nt(pick, sel_t)

    for g in range(G):
        q2 = q2s[g]
        kt_w = win[g * hd:(g + 1) * hd].astype(BF16)
        vt_w = win[(G + g) * hd:(G + g + 1) * hd].astype(BF16)
        k_n = new[:, g * hd:(g + 1) * hd].astype(BF16)
        v_n = new[:, (G + g) * hd:(G + g + 1) * hd].astype(BF16)
        s = jnp.concatenate([_dot(q2, kt_w), _nt(q2, k_n)], axis=1) + bw_ref[g * R:(g + 1) * R].reshape(rows, wk)
        jj = lax.broadcasted_iota(jnp.int32, (rows, wk), 1)
        ii = lax.broadcasted_iota(jnp.int32, (rows, wk), 0) % S
        valid = ((jj < w_buf) & (jj > ii)) | ((jj >= w_buf) & (jj - w_buf <= ii))
        p = _masked_softmax(s, valid).astype(BF16)
        o_w = _nt(p[:, :w_buf], vt_w) + _dot(p[:, w_buf:], v_n)
        for r in range(R):
            hh = g * R + r
            sl = slice(r * S, (r + 1) * S)
            ocw_ref[hh] = _gate_col(gts, hh * 3) * o_cs[g][sl] + _gate_col(gts, hh * 3 + 2) * o_w[sl]


def _nsa_sample_tables(rel_bias, past, S, n_sub, w_buf, wk):
    tb = _bias_by_distance(rel_bias) * LOG2E
    i = np.arange(S)[:, None]
    c = np.arange(n_sub)[None, :]
    idx_c = np.clip(past + i - (CMP_STRIDE * c + CMP_LEN - 1), 0, REL_MAX_DIST)
    j = np.arange(wk)[None, :]
    idx_w = np.where(j < w_buf, np.clip(w_buf + i - j, 0, REL_MAX_DIST), np.clip(i - (j - w_buf), 0, REL_MAX_DIST))
    return tb[:, idx_c], tb[:, idx_w], tb


def _nsa_sample_local(qa, kc, vc, win_t, kvw_new, gates, bias_c, bias_w, Bd, S, n_sub, n_pb):
    G, R = NSA_KV_HEADS, NSA_REP
    w_buf = win_t.shape[2]
    ovt = jnp.asarray(_overlap_np(n_sub).T, BF16)
    k_pick = min(N_SEL, n_pb + 1) - 1
    assert G * S <= LANES
    kern = functools.partial(_nsa_sample_local_kernel, n_sub=n_sub, n_pb=n_pb, k_pick=k_pick)
    hm_spec = pl.BlockSpec((NSA_HEADS, S, HEAD_DIM), lambda s: (0, s, 0))
    cmp_spec = pl.BlockSpec((G, n_sub, HEAD_DIM), lambda s: (0, s, 0))
    full = lambda a: pl.BlockSpec(a.shape, lambda s: (0,) * a.ndim)
    return pl.pallas_call(
        kern,
        out_shape=[jax.ShapeDtypeStruct((NSA_HEADS, Bd * S, HEAD_DIM), F32),
                   jax.ShapeDtypeStruct((Bd, G * R * S, LANES), F32)],
        grid=(Bd,),
        in_specs=[hm_spec, cmp_spec, cmp_spec, pl.BlockSpec((1, 2 * G * HEAD_DIM, w_buf), lambda s: (s, 0, 0)),
                  pl.BlockSpec((S, 2 * G * HEAD_DIM), lambda s: (s, 0)),
                  pl.BlockSpec((S, LANES), lambda s: (s, 0)), full(bias_c), full(bias_w), full(ovt)],
        out_specs=[hm_spec, pl.BlockSpec((1, G * R * S, LANES), lambda s: (s, 0, 0))],
        compiler_params=_cp(("arbitrary",)),
        name="nsa_sample_local",
    )(qa, kc, vc, win_t, kvw_new, gates, bias_c, bias_w, ovt)


def _nsa_sample_slc_kernel(pt_ref, *refs, P, nj):
    del pt_ref
    pages = refs[:P]
    (q_ref, sel_ref, ocw_ref, gate_ref, new_ref, blast_ref, bnew_ref, o_ref, m_ref, l_ref, acc_ref) = refs[P:]
    R, G = NSA_REP, NSA_KV_HEADS
    S = q_ref.shape[1]
    rows = R * S
    hd = HEAD_DIM
    j = pl.program_id(1)

    @pl.when(j == 0)
    def _():
        m_ref[...] = jnp.full_like(m_ref, NEG)
        l_ref[...] = jnp.zeros_like(l_ref)
        acc_ref[...] = jnp.zeros_like(acc_ref)

    def q_of(g):
        return q_ref[g * R:(g + 1) * R].reshape(rows, hd).astype(BF16)

    def update(g, s, pv):
        m_ref[g], l_ref[g], acc_ref[g] = _flash_step((m_ref[g], l_ref[g], acc_ref[g]), s, pv)

    kv = jnp.concatenate([r[0] for r in pages], axis=1)
    nk = kv.shape[1]
    e_b = lax.broadcasted_iota(jnp.int32, (LANES, nk), 0)
    e_k = lax.broadcasted_iota(jnp.int32, (LANES, nk), 1) // SEL_BLOCK
    expand = jnp.where(e_b == j * (nk // SEL_BLOCK) + e_k, 1.0, 0.0).astype(BF16)
    last = jnp.where(j == nj - 1, 1.0, 0.0)
    for g in range(G):
        rs = slice(g * rows, (g + 1) * rows)
        kt = kv[g * hd:(g + 1) * hd].astype(BF16)
        vt = kv[(G + g) * hd:(G + g + 1) * hd].astype(BF16)
        add = (_dot(sel_ref[0, rs, :].astype(BF16), expand) - 1.0) * (-NEG) + blast_ref[rs, :] * last
        update(g, _dot(q_of(g), kt) + add, lambda p: _nt(p, vt))

    @pl.when(j == nj - 1)
    def _():
        new = _pad_rows(new_ref[...], PAGE_SIZE)
        kk = lax.broadcasted_iota(jnp.int32, (rows, PAGE_SIZE), 1)
        ii = lax.broadcasted_iota(jnp.int32, (rows, PAGE_SIZE), 0) % S
        causal_add = _mask_add(kk <= ii)
        gts = gate_ref[...]
        for g in range(G):
            rs = slice(g * rows, (g + 1) * rows)
            k = new[:, g * hd:(g + 1) * hd].astype(BF16)
            v = new[:, (G + g) * hd:(G + g + 1) * hd].astype(BF16)
            update(g, _nt(q_of(g), k) + bnew_ref[rs, :] + causal_add, lambda p: _dot(p, v))
            o_s = _flash_out((m_ref[g], l_ref[g], acc_ref[g]))
            for r in range(R):
                hh = g * R + r
                o_ref[hh] = ocw_ref[hh] + _gate_col(gts, hh * 3 + 1) * o_s[r * S:(r + 1) * S]


def _nsa_sample_slc(cache_t, page_table, qa, sel, ocw, gates, kvs_new, tb, S, past):
    Bd, n_pages = page_table.shape
    G, R = NSA_KV_HEADS, NSA_REP
    P = min(16, n_pages)
    nj = n_pages // P
    nk = P * PAGE_SIZE
    i = np.arange(S)[:, None]
    k = np.arange(nk)[None, :]
    tb = tb - tb[:, REL_MAX_DIST:]
    blast = tb[:, np.clip(past + i - (past - nk + k), 0, REL_MAX_DIST)].reshape(NSA_HEADS * S, nk)
    bnew = tb[:, np.clip(i - np.arange(PAGE_SIZE)[None, :], 0, REL_MAX_DIST)].reshape(NSA_HEADS * S, PAGE_SIZE)
    hm_spec = pl.BlockSpec((NSA_HEADS, S, HEAD_DIM), lambda s, j, pt: (0, s, 0))
    full = lambda a: pl.BlockSpec(a.shape, lambda s, j, pt: (0,) * a.ndim)
    rows = R * S
    gs = pltpu.PrefetchScalarGridSpec(
        num_scalar_prefetch=1,
        grid=(Bd, nj),
        in_specs=_page_specs(P, (1, 2 * G * HEAD_DIM, PAGE_SIZE), lambda j, p: j * P + p)
        + [hm_spec, pl.BlockSpec((1, G * rows, LANES), lambda s, j, pt: (s, 0, 0)), hm_spec,
           pl.BlockSpec((S, LANES), lambda s, j, pt: (s, 0)), pl.BlockSpec((S, 256), lambda s, j, pt: (s, 0)),
           full(blast), full(bnew)],
        out_specs=hm_spec,
        scratch_shapes=[pltpu.VMEM((G, rows, 1), F32), pltpu.VMEM((G, rows, 1), F32),
                        pltpu.VMEM((G, rows, HEAD_DIM), F32)],
    )
    return pl.pallas_call(
        functools.partial(_nsa_sample_slc_kernel, P=P, nj=nj),
        out_shape=jax.ShapeDtypeStruct((NSA_HEADS, Bd * S, HEAD_DIM), F32),
        grid_spec=gs,
        compiler_params=_cp(("arbitrary", "arbitrary")),
        name="nsa_sample_slc",
    )(page_table, *([cache_t] * P), qa, sel, ocw, gates, kvs_new, blast, bnew)


def _sb_sample_kernel(pt_ref, *refs, P, nj):
    del pt_ref
    pages = refs[:P]
    q_ref, new_ref, tri_ref, rep_ref, fold_ref, o_ref, suf_ref, acc_ref, qbd_ref = refs[P:]
    H = SB_HEADS
    S = q_ref.shape[1]
    rows = H * S
    width = H * HEAD_DIM
    j = pl.program_id(1)
    tri = tri_ref[...]
    r_io = lax.broadcasted_iota(jnp.int32, (rows, width), 0) // S
    c_io = lax.broadcasted_iota(jnp.int32, (rows, width), 1) // HEAD_DIM
    own = r_io == c_io

    @pl.when(j == 0)
    def _():
        q64 = q_ref[...].reshape(rows, HEAD_DIM).astype(BF16)
        qbd = jnp.where(own, _dot(q64, rep_ref[...]), 0.0).astype(BF16)
        qbd_ref[...] = qbd
        new = _pad_rows(new_ref[...], PAGE_SIZE)
        kk = lax.broadcasted_iota(jnp.int32, (rows, PAGE_SIZE), 1)
        ii = lax.broadcasted_iota(jnp.int32, (rows, PAGE_SIZE), 0) % S
        att, suf = _sb_weights(_nt(qbd, new[:, :width].astype(BF16)), tri, jnp.zeros((rows, 1), F32), kk < ii)
        suf_ref[...] = suf
        acc_ref[...] = _dot(att, new[:, width:].astype(BF16))

    kt = jnp.concatenate([r[0, :width, :].astype(BF16) for r in pages], axis=1)
    vt = jnp.concatenate([r[0, width:, :].astype(BF16) for r in pages], axis=1)
    lsz, lk = _sb_logs(_dot(qbd_ref[...], kt))
    chunks = [lk[:, p * PAGE_SIZE:(p + 1) * PAGE_SIZE] for p in range(P)]
    within = _dot2(jnp.concatenate(chunks, axis=0), tri)
    run = suf_ref[...]
    logw = []
    for p in range(P):
        sl = slice(p * PAGE_SIZE, (p + 1) * PAGE_SIZE)
        logw.append(lsz[:, sl] + within[p * rows:(p + 1) * rows] + run)
        run = run + jnp.sum(chunks[p], axis=-1, keepdims=True)
    att = jnp.exp2(jnp.concatenate(logw, axis=1)).astype(BF16)
    acc = acc_ref[...] + _nt(att, vt)
    suf_ref[...] = run
    acc_ref[...] = acc

    @pl.when(j == nj - 1)
    def _():
        o = _exact_dot(jnp.where(own, acc, 0.0), fold_ref[...])
        o_ref[...] = o.reshape(H, S, HEAD_DIM)


def _sb_sample(cache_t, page_table, qb, kvsb_new, S):
    Bd, n_pages = page_table.shape
    H = SB_HEADS
    width = H * HEAD_DIM
    P = min(8, n_pages)
    nj = n_pages // P
    tri = _suffix_matrix(PAGE_SIZE)
    rep = jnp.asarray(np.tile(np.eye(HEAD_DIM, dtype=np.float32), (1, H)), BF16)
    fold = jnp.asarray(np.tile(np.eye(HEAD_DIM, dtype=np.float32), (H, 1)), BF16)
    hm_spec = pl.BlockSpec((H, S, HEAD_DIM), lambda s, j, pt: (0, s, 0))
    full = lambda a: pl.BlockSpec(a.shape, lambda s, j, pt: (0,) * a.ndim)
    gs = pltpu.PrefetchScalarGridSpec(
        num_scalar_prefetch=1,
        grid=(Bd, nj),
        in_specs=_page_specs(P, (1, 2 * width, PAGE_SIZE), lambda j, p: n_pages - 1 - (j * P + p))
        + [hm_spec, pl.BlockSpec((S, 2 * width), lambda s, j, pt: (s, 0)), full(tri), full(rep), full(fold)],
        out_specs=hm_spec,
        scratch_shapes=[pltpu.VMEM((H * S, 1), F32), pltpu.VMEM((H * S, width), F32), pltpu.VMEM((H * S, width), BF16)],
    )
    return pl.pallas_call(
        functools.partial(_sb_sample_kernel, P=P, nj=nj),
        out_shape=jax.ShapeDtypeStruct((H, Bd * S, HEAD_DIM), F32),
        grid_spec=gs,
        compiler_params=_cp(("arbitrary", "arbitrary")),
        name="sb_sample",
    )(page_table, *([cache_t] * P), qb, kvsb_new, tri, rep, fold)


def kernel(x_prompt, x_sample, c_prompt, c_sample, cache_sb, cache_cmp, cache_slc, cache_win, page_table, rel_bias, final_norm_g, ada_w, ada_b, norm1_g, norm2_g, w_in, cmp_pe, cmp_w1, cmp_b1, cmp_w2, out_norm_a, out_norm_b, w_out, router_group_w, router_group_b, router_expert_w, router_expert_b, expert_w1, expert_w3, expert_w2):
    assert ada_w.shape[0] == 1, "single-layer trunk"
    l = 0
    B, T, D = x_prompt.shape
    Bd, S, _ = x_sample.shape
    n_pages = page_table.shape[1]
    past = n_pages * PAGE_SIZE
    w_buf = cache_win.shape[2]
    assert S < CMP_STRIDE and w_buf == WINDOW and past >= WINDOW and 2 * n_pages <= LANES and T // SEL_BLOCK <= LANES
    G, hd = NSA_KV_HEADS, HEAD_DIM

    wi = w_in[l]
    n_gate = 3 * NSA_HEADS
    off_gate = NSA_HEADS * hd + 3 * 2 * G * hd
    w_perm = jnp.concatenate([wi[:, :off_gate], wi[:, off_gate + n_gate:], wi[:, off_gate:off_gate + n_gate],
                              jnp.zeros((D, _C_END - _C_GATE - n_gate), wi.dtype)], axis=1).astype(BF16)
    w_out_h = w_out[l].reshape(NSA_HEADS + SB_HEADS, hd, D).astype(BF16)
    wr = jnp.concatenate([router_group_w[l], router_expert_w[l],
                          jnp.zeros((D, LANES - N_GROUPS - N_EXPERTS), F32)], axis=1).astype(BF16)
    br = jnp.concatenate([router_group_b[l], router_expert_b[l],
                          jnp.zeros((LANES - N_GROUPS - N_EXPERTS,), F32)]).reshape(1, LANES)
    ga = out_norm_a[l].reshape(NSA_HEADS, 1, hd)
    gb = out_norm_b[l].reshape(SB_HEADS, 1, hd)
    n2g = norm2_g[l].reshape(1, 1, D)
    cw = _compress_weights(cmp_pe[l], cmp_w1[l], cmp_b1[l], cmp_w2[l])

    n_c = B + Bd
    c_all = jnp.concatenate([c_prompt, c_sample, jnp.zeros((-n_c % 8, D), F32)], axis=0)
    mod = _ada(c_all, ada_w[l], ada_b[l])
    mod_p = mod[:B].reshape(B, 6, 1, D)
    mod_s = mod[B:n_c].reshape(Bd, 6, 1, D)

    rp = min(256, T)
    (qa, qb, ks, vs, kw, vw, ksb, vsb, kvc, kvs, kvw, kvsb, gates) = _inproj(
        x_prompt, mod_p[:, 1], mod_p[:, 0], norm1_g[l], w_perm, 1, rp)
    kc, vc = _compress_prompt(kvc, B, T, cw)
    oa = _nsa_prompt(qa, kc, vc, ks, vs, kw, vw, gates, _nsa_prompt_tables(rel_bias), B, T)
    ob = _sb_prompt(qb, ksb, vsb, B, T)
    x1_p, h2_p, route_p, oh_p, cnt_p = _outproj(oa, ob, x_prompt, mod_p[:, 2], mod_p[:, 4], mod_p[:, 3],
                                                ga, gb, n2g, w_out_h, wr, br, 1, rp)
    st_p = (kvsb.reshape(1, B, T, 2, SB_HEADS, hd), kvc.reshape(1, B, T, 2, G, hd), kvs.reshape(1, B, T, 2, G, hd),
            kvw.reshape(B, T, 2, G, hd)[None, :, T - min(WINDOW, T):])

    ns = max(1, LANES // S)
    (qa_s, qb_s, _, _, _, _, _, _, kvc_s, kvs_s, kvw_s, kvsb_s, gates_s) = _inproj(
        x_sample, mod_s[:, 1], mod_s[:, 0], norm1_g[l], w_perm, ns, S)
    qa_s = qa_s.astype(F32)
    qb_s = qb_s.astype(F32)
    n_sub = past // CMP_STRIDE
    n_pb = past // SEL_BLOCK
    def token_minor(c):
        return jnp.transpose(c, (0, 2, 3, 4, 1)).reshape(c.shape[0], -1, c.shape[1])

    win_new = jnp.concatenate([cache_win[l], kvw_s.reshape(Bd, S, 2, G, hd)], axis=1)[:, S:]
    kc_s, vc_s = _compress_sample(cache_cmp[l], page_table, cw)
    bias_c, bias_w, tb = _nsa_sample_tables(rel_bias, past, S, n_sub, w_buf, w_buf + LANES)
    ocw, sel = _nsa_sample_local(qa_s, kc_s, vc_s, token_minor(cache_win[l]), kvw_s, gates_s, bias_c, bias_w,
                                 Bd, S, n_sub, n_pb)
    oa_s = _nsa_sample_slc(token_minor(cache_slc[l]), page_table, qa_s, sel, ocw, gates_s, kvs_s, tb, S, past)
    ob_s = _sb_sample(token_minor(cache_sb[l]), page_table, qb_s, kvsb_s, S)
    x1_s, h2_s, route_s, oh_s, cnt_s = _outproj(oa_s, ob_s, x_sample, mod_s[:, 2], mod_s[:, 4], mod_s[:, 3],
                                                ga, gb, n2g, w_out_h, wr, br, ns, S)
    st_s = (kvsb_s.reshape(1, Bd, S, 2, SB_HEADS, hd), kvc_s.reshape(1, Bd, S, 2, G, hd),
            kvs_s.reshape(1, Bd, S, 2, G, hd), win_new[None])

    n_p = B * T
    f = _moe(jnp.concatenate([h2_p, h2_s], axis=0), jnp.concatenate([route_p, route_s], axis=0),
             jnp.concatenate([oh_p, oh_s], axis=0), cnt_p + cnt_s, expert_w1[l], expert_w3[l], expert_w2[l])
    y_p = _final(x1_p.reshape(B, T, D), f[:n_p].reshape(B, T, D), mod_p[:, 5], final_norm_g, 1, rp)
    y_s = _final(x1_s.reshape(Bd, S, D), f[n_p:].reshape(Bd, S, D), mod_s[:, 5], final_norm_g, ns, S)
    return (y_p, y_s) + st_p + st_s
```
